```python
import jax
import jax.numpy as jnp
from jax import lax
import numpy as np

D_MODEL = 2048
BATCH = 8
SEQ = 4096
DEPTH = 4

CTX_LEN = 256
GRID_W = 64
HEAD_DIM = 128
A_HEADS = D_MODEL // 2 // HEAD_DIM
A_KV_HEADS = 2
B_HEADS = D_MODEL // 4 // HEAD_DIM
B_KV_HEADS = 2
F_GROUPS = 4
F_GROUP_DIM = D_MODEL // 4 // F_GROUPS
WINDOW = 128
Q_BLOCK = 128
D_FF = 5632
CONV_W = 3
ROPE_BASE = 10000.0
EPS = 1e-6
NEG_INF = -1e30
DEEPNORM_ALPHA = (2.0 * DEPTH) ** 0.25
DEEPNORM_BETA = (8.0 * DEPTH) ** -0.25

QA_W = A_HEADS * HEAD_DIM
QB_W = B_HEADS * HEAD_DIM
KA_W = A_KV_HEADS * HEAD_DIM
KB_W = B_KV_HEADS * HEAD_DIM
F_WIDTH = F_GROUPS * F_GROUP_DIM
MIX_WIDTH = QA_W + QB_W + F_WIDTH
IN_SIZES = (QA_W, QB_W, KA_W, KA_W, KB_W, KB_W, F_WIDTH)
KV_SIZES = (KA_W, KA_W, KB_W, KB_W)
IN_WIDTH = QA_W + QB_W + 2 * KA_W + 2 * KB_W + F_WIDTH
KV_START = QA_W + QB_W
KV_END = KV_START + 2 * KA_W + 2 * KB_W

kernel_name = 'hybrid_flow_backbone_parallel_heads'


def _split(t, sizes):
    out, start = [], 0
    for s in sizes:
        out.append(t[..., start:start + s])
        start += s
    return out


def _heads(t, n_heads):
    return t.reshape(t.shape[:-1] + (n_heads, HEAD_DIM))


def layer_norm(t, gain=None, bias=None):
    tf = t.astype(jnp.float32)
    mu = jnp.mean(tf, -1, keepdims=True)
    var = jnp.mean(jnp.square(tf - mu), -1, keepdims=True)
    y = (tf - mu) * lax.rsqrt(var + EPS)
    if gain is not None:
        y = y * gain.astype(jnp.float32) + bias.astype(jnp.float32)
    return y.astype(t.dtype)


def rms_norm(t, gain):
    tf = t.astype(jnp.float32)
    y = tf * lax.rsqrt(jnp.mean(tf * tf, -1, keepdims=True) + EPS) * gain.astype(jnp.float32)
    return y.astype(t.dtype)


def modulate(t, shift, scale):
    return layer_norm(t) * (1.0 + scale) + shift


def axial_rope_tables(n_tokens):
    rows = n_tokens // GRID_W
    row = jnp.repeat(jnp.arange(rows), GRID_W).astype(jnp.float32)
    col = jnp.tile(jnp.arange(GRID_W), rows).astype(jnp.float32)
    n_freq = HEAD_DIM // 4
    inv_freq = ROPE_BASE ** (-jnp.arange(n_freq, dtype=jnp.float32) / n_freq)
    ang = jnp.stack([row[:, None] * inv_freq, col[:, None] * inv_freq], axis=1)
    return jnp.cos(ang), jnp.sin(ang)


def apply_rope(t, cos, sin):
    n_freq = HEAD_DIM // 4
    tf = t.astype(jnp.float32).reshape(t.shape[:-1] + (2, 2, n_freq))
    a, b = tf[..., 0, :], tf[..., 1, :]
    c = cos[None, :, None]
    s = sin[None, :, None]
    out = jnp.stack([a * c - b * s, a * s + b * c], axis=-2)
    return out.reshape(t.shape).astype(t.dtype)


def softmax_with_sink(s, sink):
    m = jnp.maximum(jnp.max(s, -1, keepdims=True), sink)
    e = jnp.exp(s - m)
    return e / (jnp.sum(e, -1, keepdims=True) + jnp.exp(sink - m))


def context_attention(q, k, v, sink=None):
    bsz, n_ctx, n_heads, d = q.shape
    n_kv = k.shape[2]
    qg = q.reshape(bsz, n_ctx, n_kv, n_heads // n_kv, d)
    s = jnp.einsum('bqkgd,bskd->bkgqs', qg, k).astype(jnp.float32) * (d ** -0.5)
    if sink is None:
        p = jax.nn.softmax(s, -1)
    else:
        p = softmax_with_sink(s, sink.astype(jnp.float32).reshape(n_kv, -1)[None, :, :, None, None])
    o = jnp.einsum('bkgqs,bskd->bqkgd', p.astype(v.dtype), v)
    return o.reshape(bsz, n_ctx, n_heads * d)


def global_attention_latent(q, k, v, k_ctx, v_ctx):
    bsz, n_tok, n_heads, d = q.shape
    n_grp = n_heads // A_KV_HEADS
    n_blk = n_tok // Q_BLOCK
    kk = jnp.concatenate([k, k_ctx], axis=1)
    vv = jnp.concatenate([v, v_ctx], axis=1)
    qb = q.reshape(bsz, n_blk, Q_BLOCK, A_KV_HEADS, n_grp, d).transpose(1, 0, 2, 3, 4, 5)
    scale = d ** -0.5

    def block(qi):
        s = jnp.einsum('bqkgd,bskd->bkgqs', qi, kk).astype(jnp.float32) * scale
        p = jax.nn.softmax(s, -1).astype(vv.dtype)
        return jnp.einsum('bkgqs,bskd->bqkgd', p, vv)

    o = lax.map(block, qb)
    return o.transpose(1, 0, 2, 3, 4, 5).reshape(bsz, n_tok, n_heads * d)


def window_attention_latent(q, k, v, k_ctx, v_ctx, sink):
    bsz, n_tok, n_heads, d = q.shape
    n_grp = n_heads // B_KV_HEADS
    n_blk = n_tok // Q_BLOCK
    qb = q.reshape(bsz, n_blk, Q_BLOCK, B_KV_HEADS, n_grp, d)

    def band(t):
        tb = t.reshape(bsz, n_blk, Q_BLOCK, B_KV_HEADS, d)
        tp = jnp.pad(tb, ((0, 0), (1, 1), (0, 0), (0, 0), (0, 0)))
        return jnp.concatenate([tp[:, :-2], tp[:, 1:-1], tp[:, 2:]], axis=2)

    kb, vb = band(k), band(v)
    qpos = jnp.arange(n_tok).reshape(n_blk, Q_BLOCK)
    kpos = (jnp.arange(n_blk)[:, None] - 1) * Q_BLOCK + jnp.arange(3 * Q_BLOCK)[None, :]
    allowed = ((jnp.abs(qpos[:, :, None] - kpos[:, None, :]) <= WINDOW)
               & (kpos[:, None, :] >= 0) & (kpos[:, None, :] < n_tok))
    scale = d ** -0.5
    s_loc = jnp.einsum('bnqkgd,bnskd->bnkgqs', qb, kb).astype(jnp.float32) * scale
    s_loc = jnp.where(allowed[None, :, None, None], s_loc, NEG_INF)
    s_ctx = jnp.einsum('bnqkgd,bskd->bnkgqs', qb, k_ctx).astype(jnp.float32) * scale
    s = jnp.concatenate([s_loc, s_ctx], axis=-1)
    sink_b = sink.astype(jnp.float32).reshape(B_KV_HEADS, n_grp)[None, None, :, :, None, None]
    p = softmax_with_sink(s, sink_b).astype(v.dtype)
    o = (jnp.einsum('bnkgqs,bnskd->bnqkgd', p[..., :3 * Q_BLOCK], vb)
         + jnp.einsum('bnkgqs,bskd->bnqkgd', p[..., 3 * Q_BLOCK:], v_ctx))
    return o.reshape(bsz, n_tok, n_heads * d)


def fourier_mix(u, w_f):
    bsz, n_tok, _ = u.shape
    ug = u.astype(jnp.float32).reshape(bsz, n_tok, F_GROUPS, F_GROUP_DIM)
    z = jnp.fft.fft2(ug, axes=(1, 3), norm='ortho').real.astype(u.dtype)
    return jnp.einsum('bngc,gce->bnge', z, w_f).reshape(bsz, n_tok, F_WIDTH)


def conv_ffn(h, w_up, w_gate, conv_w, conv_b, w_down):
    n_tok = h.shape[1]
    u = h @ w_up
    g = h @ w_gate
    half = CONV_W // 2
    gp = jnp.pad(g, ((0, 0), (half, half), (0, 0)))
    g = conv_b + sum(gp[:, j:j + n_tok] * conv_w[j] for j in range(CONV_W))
    return (jax.nn.silu(g) * u) @ w_down


def setup_inputs(seed: int = 0) -> dict:
    key = jax.random.key(seed)
    ks = jax.random.split(key, 24)
    D = D_MODEL

    def nrm(k, shape, s):
        return jax.random.normal(k, shape, jnp.float32) * s

    return {
        'x': nrm(ks[0], (BATCH, SEQ, D), 1.0),
        'c': nrm(ks[1], (BATCH, D), 1.0),
        'ctx': nrm(ks[2], (BATCH, CTX_LEN, D), 1.0),
        'c_ctx': nrm(ks[3], (D,), 1.0),
        'w_mod': nrm(ks[4], (DEPTH, D, 6 * D), 0.5 * D ** -0.5),
        'b_mod': nrm(ks[5], (DEPTH, 6 * D), 0.02),
        'w_in': nrm(ks[6], (DEPTH, D, IN_WIDTH), D ** -0.5),
        'q_gain_a': 1.0 + nrm(ks[7], (DEPTH, HEAD_DIM), 0.02),
        'k_gain_a': 1.0 + nrm(ks[8], (DEPTH, HEAD_DIM), 0.02),
        'sink_b': nrm(ks[9], (DEPTH, B_HEADS), 0.5),
        'w_fourier': nrm(ks[10], (DEPTH, F_GROUPS, F_GROUP_DIM, F_GROUP_DIM), F_GROUP_DIM ** -0.5),
        'w_out': nrm(ks[11], (DEPTH, MIX_WIDTH, D), DEEPNORM_BETA * MIX_WIDTH ** -0.5),
        'ln1_g': 1.0 + nrm(ks[12], (DEPTH, D), 0.02),
        'ln1_b': nrm(ks[13], (DEPTH, D), 0.02),
        'w_up': nrm(ks[14], (DEPTH, D, D_FF), D ** -0.5),
        'w_gate': nrm(ks[15], (DEPTH, D, D_FF), D ** -0.5),
        'conv_w': nrm(ks[16], (DEPTH, CONV_W, D_FF), CONV_W ** -0.5),
        'conv_b': nrm(ks[17], (DEPTH, D_FF), 0.02),
        'w_down': nrm(ks[18], (DEPTH, D_FF, D), DEEPNORM_BETA * D_FF ** -0.5),
        'ln2_g': 1.0 + nrm(ks[19], (DEPTH, D), 0.02),
        'ln2_b': nrm(ks[20], (DEPTH, D), 0.02),
    }


def reference(x, c, ctx, c_ctx, w_mod, b_mod, w_in, q_gain_a, k_gain_a, sink_b, w_fourier,
              w_out, ln1_g, ln1_b, w_up, w_gate, conv_w, conv_b, w_down, ln2_g, ln2_b):
    n_tok = x.shape[1]
    cos, sin = axial_rope_tables(n_tok)
    for l in range(DEPTH):
        last = l == DEPTH - 1
        mod = (jax.nn.silu(c) @ w_mod[l] + b_mod[l])[:, None, :]
        sh1, sc1, g1, sh2, sc2, g2 = jnp.split(mod, 6, axis=-1)

        if last:
            mod_c = jax.nn.silu(c_ctx) @ w_mod[l][:, :2 * D_MODEL] + b_mod[l][:2 * D_MODEL]
            csh1, csc1 = jnp.split(mod_c, 2)
            hc = modulate(ctx, csh1, csc1)
            pc_kv = hc @ w_in[l][:, KV_START:KV_END]
        else:
            mod_c = jax.nn.silu(c_ctx) @ w_mod[l] + b_mod[l]
            csh1, csc1, cg1, csh2, csc2, cg2 = jnp.split(mod_c, 6)
            hc = modulate(ctx, csh1, csc1)
            pc = hc @ w_in[l]
            pc_kv = pc[..., KV_START:KV_END]
        kA_c, vA_c, kB_c, vB_c = _split(pc_kv, KV_SIZES)
        kA_c = rms_norm(_heads(kA_c, A_KV_HEADS), k_gain_a[l])
        vA_c = _heads(vA_c, A_KV_HEADS)
        kB_c = _heads(kB_c, B_KV_HEADS)
        vB_c = _heads(vB_c, B_KV_HEADS)

        h = modulate(x, sh1, sc1)
        qA, qB, kA, vA, kB, vB, uF = _split(h @ w_in[l], IN_SIZES)
        qA = apply_rope(rms_norm(_heads(qA, A_HEADS), q_gain_a[l]), cos, sin)
        kA = apply_rope(rms_norm(_heads(kA, A_KV_HEADS), k_gain_a[l]), cos, sin)
        qB = apply_rope(_heads(qB, B_HEADS), cos, sin)
        kB = apply_rope(_heads(kB, B_KV_HEADS), cos, sin)
        oA = global_attention_latent(qA, kA, _heads(vA, A_KV_HEADS), kA_c, vA_c)
        oB = window_attention_latent(qB, kB, _heads(vB, B_KV_HEADS), kB_c, vB_c, sink_b[l])
        oF = fourier_mix(uF, w_fourier[l])
        y = jnp.concatenate([oA, oB, oF], axis=-1) @ w_out[l]
        x = layer_norm(DEEPNORM_ALPHA * x + g1 * y, ln1_g[l], ln1_b[l])
        f = conv_ffn(modulate(x, sh2, sc2), w_up[l], w_gate[l], conv_w[l], conv_b[l], w_down[l])
        x = layer_norm(DEEPNORM_ALPHA * x + g2 * f, ln2_g[l], ln2_b[l])

        if not last:
            qA_c, qB_c, _, _, _, _, uF_c = _split(pc, IN_SIZES)
            oA_c = context_attention(rms_norm(_heads(qA_c, A_HEADS), q_gain_a[l]), kA_c, vA_c)
            oB_c = context_attention(_heads(qB_c, B_HEADS), kB_c, vB_c, sink_b[l])
            oF_c = fourier_mix(uF_c, w_fourier[l])
            yc = jnp.concatenate([oA_c, oB_c, oF_c], axis=-1) @ w_out[l]
            ctx1 = layer_norm(DEEPNORM_ALPHA * ctx + cg1 * yc, ln1_g[l], ln1_b[l])
            fc = conv_ffn(modulate(ctx1, csh2, csc2), w_up[l], w_gate[l], conv_w[l], conv_b[l], w_down[l])
            ctx = layer_norm(DEEPNORM_ALPHA * ctx1 + cg2 * fc, ln2_g[l], ln2_b[l])
    return x
```

```python
import functools
import math

import numpy as np
import jax
import jax.numpy as jnp
from jax import lax
from jax.experimental import pallas as pl
from jax.experimental.pallas import tpu as pltpu

F32 = jnp.float32
BF16 = jnp.bfloat16

HEAD_DIM = 128
A_KV_HEADS = 2
B_KV_HEADS = 2
F_GROUPS = 4
WINDOW = 128
GRID_W = 64
ROPE_BASE = 10000.0
EPS = 1e-6
NEG_INF = -1e30
MOD_ROWS = 16
VMEM_LIMIT_BYTES = 56 * 1024 * 1024


def _cparams(*sem):
    return pltpu.CompilerParams(dimension_semantics=sem, vmem_limit_bytes=VMEM_LIMIT_BYTES)


def _dot(a, b):
    return jnp.dot(a, b, preferred_element_type=F32)


def _ln(v):
    mu = jnp.mean(v, axis=-1, keepdims=True)
    d = v - mu
    var = jnp.mean(d * d, axis=-1, keepdims=True)
    return d * lax.rsqrt(var + EPS)


def _post_norm(x, y, gate, lg, lb, alpha):
    return _ln(alpha * x + gate * y) * lg + lb


def _modulate(x, sh, sc):
    return _ln(x) * (1.0 + sc) + sh


def _mod_kernel(c_ref, w_ref, b_ref, o_ref):
    a = c_ref[...]
    a = a * jax.nn.sigmoid(a)
    o_ref[0] = jnp.dot(a, w_ref[0], preferred_element_type=F32,
                       precision=lax.Precision.HIGHEST) + b_ref[0]


def _mod_all(cc, w_mod, b_mod, tn=1024):
    depth, d, n = w_mod.shape
    return pl.pallas_call(
        _mod_kernel,
        grid=(depth, n // tn),
        in_specs=[pl.BlockSpec((MOD_ROWS, d), lambda l, j: (0, 0)),
                  pl.BlockSpec((1, d, tn), lambda l, j: (l, 0, j)),
                  pl.BlockSpec((1, 1, tn), lambda l, j: (l, 0, j))],
        out_specs=pl.BlockSpec((1, MOD_ROWS, tn), lambda l, j: (l, 0, j)),
        out_shape=jax.ShapeDtypeStruct((depth, MOD_ROWS, n), F32),
        compiler_params=_cparams("parallel", "parallel"),
        name="mod_all",
    )(cc, w_mod, b_mod.reshape(depth, 1, n))


def _fold_kernel(cs_ref, w_ref, o_ref):
    o_ref[0] = jnp.dot(cs_ref[...], w_ref[0], preferred_element_type=F32,
                       precision=lax.Precision.HIGHEST)


def _fourier_fold(w_fourier):
    depth, groups, c, _ = w_fourier.shape
    k = np.arange(c)
    ang = 2.0 * np.pi * np.outer(k, k) / c
    cs = np.concatenate([np.cos(ang), -np.sin(ang)], axis=0) / math.sqrt(c)
    out = pl.pallas_call(
        _fold_kernel,
        grid=(depth * groups,),
        in_specs=[pl.BlockSpec((2 * c, c), lambda i: (0, 0)),
                  pl.BlockSpec((1, c, c), lambda i: (i, 0, 0))],
        out_specs=pl.BlockSpec((1, 2 * c, c), lambda i: (i, 0, 0)),
        out_shape=jax.ShapeDtypeStruct((depth * groups, 2 * c, c), F32),
        compiler_params=_cparams("parallel"),
        name="fourier_fold",
    )(jnp.asarray(cs, F32), w_fourier.reshape(depth * groups, c, c))
    out = out.reshape(depth, groups, 2, c, c).transpose(0, 1, 3, 2, 4).reshape(depth, groups, c, 2 * c)
    return out.astype(BF16)


def _ln_mod_kernel(x_ref, sh_ref, sc_ref, h_ref):
    h_ref[...] = _modulate(x_ref[...], sh_ref[0], sc_ref[0]).astype(h_ref.dtype)


def _mod_spec(d, chunk, row_fn):
    return pl.BlockSpec((1, 1, d), lambda i, *_: (row_fn(i), 0, chunk))


def _ln_mod(x_all, mod3, row_fn, tm):
    n, d = x_all.shape
    return pl.pallas_call(
        _ln_mod_kernel,
        grid=(n // tm,),
        in_specs=[pl.BlockSpec((tm, d), lambda i: (i, 0)),
                  _mod_spec(d, 0, row_fn), _mod_spec(d, 1, row_fn)],
        out_specs=pl.BlockSpec((tm, d), lambda i: (i, 0)),
        out_shape=jax.ShapeDtypeStruct((n, d), BF16),
        compiler_params=_cparams("parallel"),
        name="ln_mod",
    )(x_all, mod3, mod3)


def _rope(t, cos, s1, s2):
    return t * cos + pltpu.roll(t, HEAD_DIM - 32, 1) * s1 + pltpu.roll(t, 32, 1) * s2


def _rms(t, gain):
    return t * lax.rsqrt(jnp.mean(t * t, axis=-1, keepdims=True) + EPS) * gain


def _win_kernel(h_ref, w_ref, qg_ref, kg_ref, cos_ref, s1_ref, s2_ref, p_ref, vt_ref, *,
                block_kinds, tm, tn, scale):
    j = pl.program_id(1)
    acc = _dot(h_ref[...], w_ref[...])
    n_chunks = tn // HEAD_DIM

    def emit(kinds):
        cos, s1, s2 = cos_ref[...], s1_ref[...], s2_ref[...]
        v_chunks = [c for c, kind in enumerate(kinds) if kind == "v"]
        for c, kind in enumerate(kinds):
            t = acc[:, c * HEAD_DIM:(c + 1) * HEAD_DIM]
            if kind == "qa":
                t = _rope(_rms(t, qg_ref[...]), cos, s1, s2) * scale
            elif kind == "qb":
                t = _rope(t, cos, s1, s2) * scale
            elif kind == "ka":
                t = _rope(_rms(t, kg_ref[...]), cos, s1, s2)
            elif kind == "kb":
                t = _rope(t, cos, s1, s2)
            p_ref[:, c * HEAD_DIM:(c + 1) * HEAD_DIM] = t.astype(p_ref.dtype)
        if v_chunks:
            for r in range(tm // HEAD_DIM):
                for n_v, c in enumerate(v_chunks):
                    t = acc[r * HEAD_DIM:(r + 1) * HEAD_DIM, c * HEAD_DIM:(c + 1) * HEAD_DIM]
                    vt_ref[r, n_v * HEAD_DIM:(n_v + 1) * HEAD_DIM, :] = t.T.astype(vt_ref.dtype)

    distinct = []
    for kinds in block_kinds:
        if kinds not in distinct:
            distinct.append(kinds)
    for kinds in distinct:
        js = [jj for jj, kk in enumerate(block_kinds) if kk == kinds]
        cond = functools.reduce(jnp.logical_or, [j == jj for jj in js])
        pl.when(cond)(functools.partial(emit, kinds))


def _in_proj(h, w, qg, kg, cos, s1, s2, *, kinds, rope_fn, tm, tn, scale):
    n, d = h.shape
    in_w = w.shape[1]
    per = tn // HEAD_DIM
    block_kinds = tuple(tuple(kinds[b * per:(b + 1) * per]) for b in range(in_w // tn))
    v_blocks = [b for b, kk in enumerate(block_kinds) if "v" in kk]
    assert len(v_blocks) == 2 and v_blocks[1] == v_blocks[0] + 1
    assert all(kk.count("v") == 2 for kk in (block_kinds[v_blocks[0]], block_kinds[v_blocks[1]]))
    jv = v_blocks[0]
    kern = functools.partial(_win_kernel, block_kinds=block_kinds, tm=tm, tn=tn, scale=scale)
    tab = pl.BlockSpec((tm, HEAD_DIM), lambda i, j: (rope_fn(i), 0))
    vec = pl.BlockSpec((1, HEAD_DIM), lambda i, j: (0, 0))
    return pl.pallas_call(
        kern,
        grid=(n // tm, in_w // tn),
        in_specs=[pl.BlockSpec((tm, d), lambda i, j: (i, 0)),
                  pl.BlockSpec((d, tn), lambda i, j: (0, j)),
                  vec, vec, tab, tab, tab],
        out_specs=[pl.BlockSpec((tm, tn), lambda i, j: (i, j)),
                   pl.BlockSpec((tm // HEAD_DIM, 2 * HEAD_DIM, HEAD_DIM),
                                lambda i, j: (i, jnp.clip(j - jv, 0, 1), 0))],
        out_shape=[jax.ShapeDtypeStruct((n, in_w), BF16),
                   jax.ShapeDtypeStruct((n // HEAD_DIM, 4 * HEAD_DIM, HEAD_DIM), BF16)],
        compiler_params=_cparams("parallel", "arbitrary"),
        name="in_proj",
    )(h, w, qg, kg, cos, s1, s2)


def _scores_t(k, q):
    return lax.dot_general(k, q, (((1,), (1,)), ((), ())), preferred_element_type=F32)


def _vt_tile(vt_ref, first, count):
    tiles = [vt_ref[first + t] for t in range(count)]
    return tiles[0] if count == 1 else jnp.concatenate(tiles, axis=1)


def _online_first(s, vt):
    m = jnp.max(s, axis=0, keepdims=True)
    p = jnp.exp(s - m)
    return m, jnp.sum(p, axis=0, keepdims=True), _dot(vt, p.astype(BF16))


def _online_step(s, vt, m, l, acc):
    m_new = jnp.maximum(m, jnp.max(s, axis=0, keepdims=True))
    p = jnp.exp(s - m_new)
    alpha = jnp.exp(m - m_new)
    return (m_new, alpha * l + jnp.sum(p, axis=0, keepdims=True),
            alpha * acc + _dot(vt, p.astype(BF16)))


def _global_attn_kernel(*refs, group, tk, n_chunks, ctx_len):
    if n_chunks:
        q_ref, k_ref, vt_ref, kc_ref, vtc_ref, o_ref = refs
    else:
        q_ref, kc_ref, vtc_ref, o_ref = refs
    sub = tk // HEAD_DIM
    for g in range(group):
        cols = slice(g * HEAD_DIM, (g + 1) * HEAD_DIM)
        q = q_ref[:, cols]
        carry = _online_first(_scores_t(kc_ref[...], q), _vt_tile(vtc_ref, 0, ctx_len // HEAD_DIM))
        if n_chunks:
            def body(c, carry):
                k = k_ref[pl.ds(pl.multiple_of(c * tk, tk), tk), :]
                return _online_step(_scores_t(k, q), _vt_tile(vt_ref, c * sub, sub), *carry)
            carry = lax.fori_loop(0, n_chunks, body, carry)
        _, l, acc = carry
        o_ref[:, cols] = (acc / l).T.astype(o_ref.dtype)


def _window_attn_kernel(*refs, group, tq, seq, ctx_len, local):
    if local:
        sink_ref, q_ref, k_ref, vt_ref, kc_ref, vtc_ref, o_ref = refs
    else:
        sink_ref, q_ref, kc_ref, vtc_ref, o_ref = refs
    kv = pl.program_id(1)
    qi = pl.program_id(2)
    sub = tq // HEAD_DIM
    n_sub = seq // HEAD_DIM

    def masked(s, k_start, valid):
        qpos = qi * tq + lax.broadcasted_iota(jnp.int32, s.shape, 1)
        kpos = k_start + lax.broadcasted_iota(jnp.int32, s.shape, 0)
        ok = (jnp.abs(qpos - kpos) <= WINDOW) & valid
        return jnp.where(ok, s, NEG_INF)

    for g in range(group):
        cols = slice(g * HEAD_DIM, (g + 1) * HEAD_DIM)
        q = q_ref[:, cols]
        carry = _online_first(_scores_t(kc_ref[...], q), _vt_tile(vtc_ref, 0, ctx_len // HEAD_DIM))
        if local:
            first = qi * sub
            prev = jnp.maximum(first - 1, 0)
            nxt = jnp.minimum(first + sub, n_sub - 1)
            k = k_ref[pl.ds(pl.multiple_of(prev * HEAD_DIM, HEAD_DIM), HEAD_DIM), :]
            s = masked(_scores_t(k, q), (first - 1) * HEAD_DIM, first > 0)
            carry = _online_step(s, _vt_tile(vt_ref, prev, 1), *carry)
            k = k_ref[pl.ds(pl.multiple_of(first * HEAD_DIM, tq), tq), :]
            s = masked(_scores_t(k, q), first * HEAD_DIM, True)
            carry = _online_step(s, _vt_tile(vt_ref, first, sub), *carry)
            k = k_ref[pl.ds(pl.multiple_of(nxt * HEAD_DIM, HEAD_DIM), HEAD_DIM), :]
            s = masked(_scores_t(k, q), (first + sub) * HEAD_DIM, first + sub < n_sub)
            carry = _online_step(s, _vt_tile(vt_ref, nxt, 1), *carry)
        m, l, acc = carry
        sink = sink_ref[kv * group + g]
        m_fin = jnp.maximum(m, sink)
        w = jnp.exp(m - m_fin)
        denom = l * w + jnp.exp(sink - m_fin)
        o_ref[:, cols] = (acc * (w / denom)).T.astype(o_ref.dtype)


def _attention(p, vt3, out_prev, sink, *, window, latent_queries, batch, seq, ctx_len, q_off, k_off,
               v_row, group, out_width, tq, tk):
    n_all = p.shape[0]
    gw = group * HEAD_DIM
    ctx_blk0 = batch * seq // ctx_len
    kc_spec = pl.BlockSpec((ctx_len, HEAD_DIM), lambda b, kv, qi, *_: (ctx_blk0 + b, k_off // HEAD_DIM + kv))
    vtc_spec = pl.BlockSpec((ctx_len // HEAD_DIM, HEAD_DIM, HEAD_DIM),
                            lambda b, kv, qi, *_: (ctx_blk0 + b, v_row + kv, 0))
    if latent_queries:
        n_q = seq // tq
        q_blk = lambda b, kv, qi, *_: (b * n_q + qi, q_off // gw + kv)
        lat_specs = [pl.BlockSpec((seq, HEAD_DIM), lambda b, kv, qi, *_: (b, k_off // HEAD_DIM + kv)),
                     pl.BlockSpec((seq // HEAD_DIM, HEAD_DIM, HEAD_DIM), lambda b, kv, qi, *_: (b, v_row + kv, 0))]
        lat_args = [p, vt3]
    else:
        tq = ctx_len
        n_q = 1
        q_blk = lambda b, kv, qi, *_: (ctx_blk0 + b, q_off // gw + kv)
        lat_specs, lat_args = [], []
    q_spec = pl.BlockSpec((tq, gw), q_blk)
    o_spec = pl.BlockSpec((tq, gw), lambda b, kv, qi, *_: (q_blk(b, kv, qi)[0], kv))
    if window:
        kern = functools.partial(_window_attn_kernel, group=group, tq=tq, seq=seq, ctx_len=ctx_len,
                                 local=latent_queries)
        pre_specs = [pl.BlockSpec(memory_space=pltpu.SMEM)]
        pre_args = [sink]
    else:
        kern = functools.partial(_global_attn_kernel, group=group, tk=tk,
                                 n_chunks=seq // tk if latent_queries else 0, ctx_len=ctx_len)
        pre_specs, pre_args = [], []
    in_specs = pre_specs + [q_spec] + lat_specs + [kc_spec, vtc_spec]
    args = pre_args + [p] + lat_args + [p, vt3]
    aliases = {}
    if out_prev is not None:
        in_specs.append(pl.BlockSpec(memory_space=pl.ANY))
        args.append(out_prev)
        aliases = {len(args) - 1: 0}
        body = kern

        def kern(*refs):
            return body(*refs[:-2], refs[-1])
    return pl.pallas_call(
        kern,
        grid=(batch, A_KV_HEADS, n_q),
        in_specs=in_specs,
        out_specs=o_spec,
        out_shape=jax.ShapeDtypeStruct((n_all, out_width), BF16),
        input_output_aliases=aliases,
        compiler_params=_cparams("parallel", "parallel", "arbitrary"),
        name=("window" if window else "global") + ("_lat" if latent_queries else "_ctx"),
    )(*args)


def _chan_kernel(u_ref, w_ref, v_ref):
    r = _dot(u_ref[...], w_ref[0])
    c = r.shape[1] // 2
    v_ref[0] = r[:, :c].astype(v_ref.dtype)
    v_ref[1] = r[:, c:].astype(v_ref.dtype)


def _fourier_channels(p, wfold_l, f_off, tm):
    n = p.shape[0]
    groups, c, _ = wfold_l.shape
    return pl.pallas_call(
        _chan_kernel,
        grid=(n // tm, groups),
        in_specs=[pl.BlockSpec((tm, c), lambda i, g: (i, f_off // c + g)),
                  pl.BlockSpec((1, c, 2 * c), lambda i, g: (g, 0, 0))],
        out_specs=pl.BlockSpec((2, tm, c), lambda i, g: (0, i, g)),
        out_shape=jax.ShapeDtypeStruct((2, n, groups * c), BF16),
        compiler_params=_cparams("parallel", "parallel"),
        name="fourier_channels",
    )(p, wfold_l)


def _dft1_kernel(w_ref, v_ref, y_ref):
    x = jnp.concatenate([v_ref[0], v_ref[1]], axis=0)
    y = _dot(w_ref[...], x)
    n1 = y.shape[0] // 2
    y_ref[0] = y[:n1].astype(y_ref.dtype)
    y_ref[1] = y[n1:].astype(y_ref.dtype)


def _dft2_kernel(m_ref, y_ref, o_ref, *, kb, cols):
    for kk in range(kb):
        x = jnp.concatenate([y_ref[0, kk], y_ref[1, kk]], axis=0)
        o_ref[:, kk * cols:(kk + 1) * cols] = _dot(m_ref[kk], x).astype(o_ref.dtype)


def _dense_dft_kernel(m_ref, v_ref, prev_ref, o_ref):
    del prev_ref
    x = jnp.concatenate([v_ref[0], v_ref[1]], axis=0)
    o_ref[...] = _dot(m_ref[...], x).astype(o_ref.dtype)


def _dft_tables(seq, ctx_len):
    n1 = n2 = int(round(math.sqrt(seq)))
    assert n1 * n2 == seq
    a1 = 2.0 * np.pi * np.outer(np.arange(n1), np.arange(n1)) / n1
    c1, s1 = np.cos(a1), np.sin(a1)
    w1 = np.block([[c1, s1], [-s1, c1]])
    k = np.arange(n1)[:, None, None] + n1 * np.arange(n2)[None, :, None]
    a2 = 2.0 * np.pi * ((k * np.arange(n2)[None, None, :]) % seq) / seq
    m2 = np.concatenate([np.cos(a2), np.sin(a2)], axis=2) / math.sqrt(seq)
    ac = 2.0 * np.pi * np.outer(np.arange(ctx_len), np.arange(ctx_len)) / ctx_len
    mc = np.concatenate([np.cos(ac), np.sin(ac)], axis=1) / math.sqrt(ctx_len)
    return n1, n2, jnp.asarray(w1, BF16), jnp.asarray(m2, BF16), jnp.asarray(mc, BF16)


def _fourier_positions(v, tables, *, batch, seq, ctx_len, with_ctx, tn=4096, kb=8):
    n1, n2, w1, m2, mc = tables
    _, n_all, cols = v.shape
    lanes = n2 * cols
    tn = min(tn, lanes)
    kb = min(kb, n1)
    v2 = v.reshape(2, n_all // n2, lanes)
    y = pl.pallas_call(
        _dft1_kernel,
        grid=(batch, lanes // tn),
        in_specs=[pl.BlockSpec((2 * n1, 2 * n1), lambda b, j: (0, 0)),
                  pl.BlockSpec((2, n1, tn), lambda b, j: (0, b, j))],
        out_specs=pl.BlockSpec((2, n1, tn), lambda b, j: (0, b, j)),
        out_shape=jax.ShapeDtypeStruct((2, batch * n1, lanes), BF16),
        compiler_params=_cparams("parallel", "parallel"),
        name="dft_stage1",
    )(w1, v2)
    y4 = y.reshape(2, batch * n1, n2, cols)
    groups = n1 // kb
    o = pl.pallas_call(
        functools.partial(_dft2_kernel, kb=kb, cols=cols),
        grid=(batch, groups),
        in_specs=[pl.BlockSpec((kb, n2, 2 * n2), lambda b, kg: (kg, 0, 0)),
                  pl.BlockSpec((2, kb, n2, cols), lambda b, kg: (0, b * groups + kg, 0, 0))],
        out_specs=pl.BlockSpec((n2, kb * cols), lambda b, kg: (b, kg)),
        out_shape=jax.ShapeDtypeStruct((n_all // n1, n1 * cols), BF16),
        compiler_params=_cparams("parallel", "parallel"),
        name="dft_stage2",
    )(m2, y4)
    o = o.reshape(n_all, cols)
    if not with_ctx:
        return o
    blk0 = batch * seq // ctx_len
    return pl.pallas_call(
        _dense_dft_kernel,
        grid=(batch,),
        in_specs=[pl.BlockSpec((ctx_len, 2 * ctx_len), lambda b: (0, 0)),
                  pl.BlockSpec((2, ctx_len, cols), lambda b: (0, blk0 + b, 0)),
                  pl.BlockSpec(memory_space=pl.ANY)],
        out_specs=pl.BlockSpec((ctx_len, cols), lambda b: (blk0 + b, 0)),
        out_shape=jax.ShapeDtypeStruct((n_all, cols), BF16),
        input_output_aliases={2: 0},
        compiler_params=_cparams("parallel"),
        name="dft_ctx",
    )(mc, v, o)


def _out_proj_kernel(oa_ref, ob_ref, of_ref, wa_ref, wb_ref, wf_ref, x_ref, g_ref, lg_ref, lb_ref,
                     sh_ref, sc_ref, xo_ref, ho_ref, *, alpha):
    y = _dot(oa_ref[...], wa_ref[...]) + _dot(ob_ref[...], wb_ref[...]) + _dot(of_ref[...], wf_ref[...])
    xn = _post_norm(x_ref[...], y, g_ref[0], lg_ref[...], lb_ref[...], alpha)
    xo_ref[...] = xn
    ho_ref[...] = _modulate(xn, sh_ref[0], sc_ref[0]).astype(ho_ref.dtype)


def _out_proj(oa, ob, of, w_out, x_all, mod3, lg, lb, *, row_fn, n_rows, tm, alpha):
    n_all, d = x_all.shape
    wa, wb, wf = oa.shape[1], ob.shape[1], of.shape[1]
    assert wa % wb == 0 and wb == wf
    row = lambda i: (i, 0)
    vec = pl.BlockSpec((1, d), lambda i: (0, 0))
    return pl.pallas_call(
        functools.partial(_out_proj_kernel, alpha=alpha),
        grid=(n_rows // tm,),
        in_specs=[pl.BlockSpec((tm, wa), row), pl.BlockSpec((tm, wb), row), pl.BlockSpec((tm, wf), row),
                  pl.BlockSpec((wa, d), lambda i: (0, 0)),
                  pl.BlockSpec((wb, d), lambda i: (wa // wb, 0)),
                  pl.BlockSpec((wf, d), lambda i: (wa // wb + 1, 0)),
                  pl.BlockSpec((tm, d), row),
                  _mod_spec(d, 2, row_fn), vec, vec, _mod_spec(d, 3, row_fn), _mod_spec(d, 4, row_fn)],
        out_specs=[pl.BlockSpec((tm, d), row), pl.BlockSpec((tm, d), row)],
        out_shape=[jax.ShapeDtypeStruct((n_all, d), F32), jax.ShapeDtypeStruct((n_all, d), BF16)],
        compiler_params=_cparams("parallel"),
        name="out_proj",
    )(oa, ob, of, w_out, w_out, w_out, x_all, mod3, lg, lb, mod3, mod3)


HALO = 16


def _ffn_kernel(hp_ref, hm_ref, hn_ref, wu_ref, wg_ref, cw_ref, cb_ref, wd_ref, x_ref, g_ref, lg_ref,
                lb_ref, *rest, tm, n_lat_tiles, seq, ctx_len, alpha, emit_h):
    if emit_h:
        sh_ref, sc_ref, xo_ref, ho_ref, hext_ref, acc_ref = rest
    else:
        xo_ref, hext_ref, acc_ref = rest
    i = pl.program_id(0)
    j = pl.program_id(1)

    @pl.when(j == 0)
    def _():
        hext_ref[0:HALO] = hp_ref[...]
        hext_ref[HALO:HALO + tm] = hm_ref[...]
        hext_ref[HALO + tm:] = hn_ref[...]

    gext = _dot(hext_ref[...], wg_ref[...])
    u = _dot(hm_ref[...], wu_ref[...])
    g_prev = gext[HALO - 1:HALO - 1 + tm]
    g_cur = gext[HALO:HALO + tm]
    g_next = gext[HALO + 1:HALO + 1 + tm]
    period = jnp.where(i < n_lat_tiles, seq, ctx_len)
    pos = (i * tm + lax.broadcasted_iota(jnp.int32, u.shape, 0)) & (period - 1)
    g_prev = jnp.where(pos == 0, 0.0, g_prev)
    g_next = jnp.where(pos == period - 1, 0.0, g_next)
    cw = cw_ref[...]
    gc = cb_ref[...] + cw[0:1] * g_prev + cw[1:2] * g_cur + cw[2:3] * g_next
    act = (gc * jax.nn.sigmoid(gc) * u).astype(BF16)
    part = _dot(act, wd_ref[...])

    @pl.when(j == 0)
    def _():
        acc_ref[...] = part

    @pl.when(j > 0)
    def _():
        acc_ref[...] += part

    @pl.when(j == pl.num_programs(1) - 1)
    def _():
        xn = _post_norm(x_ref[...], acc_ref[...], g_ref[0], lg_ref[...], lb_ref[...], alpha)
        xo_ref[...] = xn
        if emit_h:
            ho_ref[...] = _modulate(xn, sh_ref[0], sc_ref[0]).astype(ho_ref.dtype)


def _conv_ffn(h, w_up, w_gate, conv_w, conv_b, w_down, x_all, mod3, lg, lb, mod3_next, *, row_fn,
              n_rows, n_out_rows, tm, tf, seq, ctx_len, n_lat_tiles, alpha):
    n_all, d = x_all.shape
    d_ff = w_up.shape[1]
    emit_h = mod3_next is not None
    per = tm // HALO
    last_halo = n_rows // HALO - 1
    row = lambda i, j: (i, 0)
    vec = pl.BlockSpec((1, d), lambda i, j: (0, 0))
    in_specs = [pl.BlockSpec((HALO, d), lambda i, j: (jnp.maximum(i * per - 1, 0), 0)),
                pl.BlockSpec((tm, d), row),
                pl.BlockSpec((HALO, d), lambda i, j: (jnp.minimum((i + 1) * per, last_halo), 0)),
                pl.BlockSpec((d, tf), lambda i, j: (0, j)),
                pl.BlockSpec((d, tf), lambda i, j: (0, j)),
                pl.BlockSpec((conv_w.shape[0], tf), lambda i, j: (0, j)),
                pl.BlockSpec((1, tf), lambda i, j: (0, j)),
                pl.BlockSpec((tf, d), lambda i, j: (j, 0)),
                pl.BlockSpec((tm, d), row),
                _mod_spec(d, 5, row_fn), vec, vec]
    args = [h, h, h, w_up, w_gate, conv_w, conv_b, w_down, x_all, mod3, lg, lb]
    out_specs = [pl.BlockSpec((tm, d), row)]
    out_shape = [jax.ShapeDtypeStruct((n_out_rows, d), F32)]
    if emit_h:
        in_specs += [_mod_spec(d, 0, row_fn), _mod_spec(d, 1, row_fn)]
        args += [mod3_next, mod3_next]
        out_specs.append(pl.BlockSpec((tm, d), row))
        out_shape.append(jax.ShapeDtypeStruct((n_out_rows, d), BF16))
    kern = functools.partial(_ffn_kernel, tm=tm, n_lat_tiles=n_lat_tiles, seq=seq, ctx_len=ctx_len,
                             alpha=alpha, emit_h=emit_h)
    return pl.pallas_call(
        kern,
        grid=(n_rows // tm, d_ff // tf),
        in_specs=in_specs,
        out_specs=out_specs,
        out_shape=out_shape,
        scratch_shapes=[pltpu.VMEM((tm + 2 * HALO, d), BF16), pltpu.VMEM((tm, d), F32)],
        compiler_params=_cparams("parallel", "arbitrary"),
        name="conv_ffn",
    )(*args)


def _rope_tables(seq, pad_rows):
    rows = seq // GRID_W
    row = jnp.repeat(jnp.arange(rows), GRID_W).astype(F32)
    col = jnp.tile(jnp.arange(GRID_W), rows).astype(F32)
    n_freq = HEAD_DIM // 4
    inv_freq = ROPE_BASE ** (-jnp.arange(n_freq, dtype=F32) / n_freq)
    ar, ac = row[:, None] * inv_freq, col[:, None] * inv_freq
    zero = jnp.zeros_like(ar)
    cos = jnp.concatenate([jnp.cos(ar), jnp.cos(ar), jnp.cos(ac), jnp.cos(ac)], axis=1)
    s1 = jnp.concatenate([-jnp.sin(ar), zero, -jnp.sin(ac), zero], axis=1)
    s2 = jnp.concatenate([zero, jnp.sin(ar), zero, jnp.sin(ac)], axis=1)
    pad = lambda t, v: jnp.concatenate([t, jnp.full((pad_rows, HEAD_DIM), v, F32)], axis=0)
    return pad(cos, 1.0), pad(s1, 0.0), pad(s2, 0.0)


def _forward(x, c, ctx, c_ctx, w_mod, b_mod, w_in, q_gain_a, k_gain_a, sink_b, w_fourier, w_out,
             ln1_g, ln1_b, w_up, w_gate, conv_w, conv_b, w_down, ln2_g, ln2_b,
             *, tm=512, tm_in=1024, tn_in=512, tf=512, tq=512, tk=512):
    batch, seq, d = x.shape
    ctx_len = ctx.shape[1]
    depth = w_mod.shape[0]
    n_lat, n_ctx = batch * seq, batch * ctx_len
    n_all = n_lat + n_ctx
    qa_w, qb_w, f_w = d // 2, d // 4, d // 4
    ka_w, kb_w = A_KV_HEADS * HEAD_DIM, B_KV_HEADS * HEAD_DIM
    assert f_w == F_GROUPS * HEAD_DIM and batch + 1 <= MOD_ROWS
    assert seq & (seq - 1) == 0 and ctx_len & (ctx_len - 1) == 0
    widths = (("qa", qa_w), ("qb", qb_w), ("ka", ka_w), ("v", ka_w), ("kb", kb_w), ("v", kb_w), ("f", f_w))
    kinds = tuple(kind for kind, w in widths for _ in range(w // HEAD_DIM))
    qb_off = qa_w
    ka_off = qa_w + qb_w
    kb_off = ka_off + 2 * ka_w
    f_off = kb_off + 2 * kb_w
    alpha = (2.0 * depth) ** 0.25
    scale = HEAD_DIM ** -0.5

    def row_fn_for(t):
        return lambda i: jnp.where(i < n_lat // t, i // (seq // t), batch)

    def rope_fn(i):
        return jnp.where(i < n_lat // tm_in, i % (seq // tm_in), seq // tm_in)

    x_all = jnp.concatenate([x.reshape(n_lat, d), ctx.reshape(n_ctx, d)], axis=0)
    cc = jnp.concatenate([c, c_ctx[None], jnp.zeros((MOD_ROWS - batch - 1, d), F32)], axis=0)
    mod = _mod_all(cc, w_mod, b_mod)
    wfold = _fourier_fold(w_fourier)
    cos, s1, s2 = _rope_tables(seq, tm_in)
    tables = _dft_tables(seq, ctx_len)
    w_in_b, w_out_b = w_in.astype(BF16), w_out.astype(BF16)
    w_up_b, w_gate_b, w_down_b = w_up.astype(BF16), w_gate.astype(BF16), w_down.astype(BF16)
    mod3 = [mod[l].reshape(MOD_ROWS, 1, 6 * d) for l in range(depth)]
    vec = lambda t: t.reshape(1, -1)

    h = _ln_mod(x_all, mod3[0], row_fn_for(tm), tm)
    x_cur = x_all
    for l in range(depth):
        last = l == depth - 1
        p, vt3 = _in_proj(h, w_in_b[l], vec(q_gain_a[l]), vec(k_gain_a[l]), cos, s1, s2, kinds=kinds,
                          rope_fn=rope_fn, tm=tm_in, tn=tn_in, scale=scale)
        common = dict(batch=batch, seq=seq, ctx_len=ctx_len, tq=tq, tk=tk)
        ga = dict(window=False, q_off=0, k_off=ka_off, v_row=0, group=qa_w // ka_w, out_width=qa_w, **common)
        gb = dict(window=True, q_off=qb_off, k_off=kb_off, v_row=A_KV_HEADS, group=qb_w // kb_w,
                  out_width=qb_w, **common)
        o_a = _attention(p, vt3, None, None, latent_queries=True, **ga)
        o_b = _attention(p, vt3, None, sink_b[l], latent_queries=True, **gb)
        v = _fourier_channels(p, wfold[l], f_off, tm)
        if not last:
            o_a = _attention(p, vt3, o_a, None, latent_queries=False, **ga)
            o_b = _attention(p, vt3, o_b, sink_b[l], latent_queries=False, **gb)
        o_f = _fourier_positions(v, tables, batch=batch, seq=seq, ctx_len=ctx_len, with_ctx=not last)
        n_rows = n_lat if last else n_all
        x_mid, h_mid = _out_proj(o_a, o_b, o_f, w_out_b[l], x_cur, mod3[l], vec(ln1_g[l]), vec(ln1_b[l]),
                                 row_fn=row_fn_for(tm), n_rows=n_rows, tm=tm, alpha=alpha)
        outs = _conv_ffn(h_mid, w_up_b[l], w_gate_b[l], conv_w[l], vec(conv_b[l]), w_down_b[l], x_mid,
                         mod3[l], vec(ln2_g[l]), vec(ln2_b[l]), None if last else mod3[l + 1],
                         row_fn=row_fn_for(tm), n_rows=n_rows, n_out_rows=n_rows, tm=tm, tf=tf, seq=seq,
                         ctx_len=ctx_len, n_lat_tiles=n_lat // tm, alpha=alpha)
        if last:
            x_cur = outs[0]
        else:
            x_cur, h = outs
    return x_cur.reshape(batch, seq, d)


def kernel(x, c, ctx, c_ctx, w_mod, b_mod, w_in, q_gain_a, k_gain_a, sink_b, w_fourier, w_out, ln1_g, ln1_b,
           w_up, w_gate, conv_w, conv_b, w_down, ln2_g, ln2_b):
    return _forward(x, c, ctx, c_ctx, w_mod, b_mod, w_in, q_gain_a, k_gain_a, sink_b, w_fourier, w_out,
                    ln1_g, ln1_b, w_up, w_gate, conv_w, conv_b, w_down, ln2_g, ln2_b)
```

```python
import functools
import math

import numpy as np
import jax
import jax.numpy as jnp
from jax import lax
from jax.experimental import pallas as pl
from jax.experimental.pallas import tpu as pltpu

F32 = jnp.float32
BF16 = jnp.bfloat16

HEAD_DIM = 128
A_KV_HEADS = 2
B_KV_HEADS = 2
F_GROUPS = 4
WINDOW = 128
GRID_W = 64
ROPE_BASE = 10000.0
EPS = 1e-6
NEG_INF = -1e30
LOG2_E = math.log2(math.e)
MOD_ROWS = 16
VMEM_LIMIT_BYTES = 56 * 1024 * 1024


def _cparams(*sem):
    return pltpu.CompilerParams(dimension_semantics=sem, vmem_limit_bytes=VMEM_LIMIT_BYTES)


def _dot(a, b):
    return jnp.dot(a, b, preferred_element_type=F32)


def _ln(v):
    mu = jnp.mean(v, axis=-1, keepdims=True)
    d = v - mu
    var = jnp.mean(d * d, axis=-1, keepdims=True)
    return d * lax.rsqrt(var + EPS)


def _post_norm(x, y, gate, lg, lb, alpha):
    return _ln(alpha * x + gate * y) * lg + lb


def _modulate(x, sh, sc):
    return _ln(x) * (1.0 + sc) + sh


def _mod_kernel(c_ref, w_ref, b_ref, o_ref):
    a = c_ref[...]
    a = a * jax.nn.sigmoid(a)
    o_ref[0] = jnp.dot(a, w_ref[0], preferred_element_type=F32,
                       precision=lax.Precision.HIGHEST) + b_ref[0]


def _mod_all(cc, w_mod, b_mod, tn=1024):
    depth, d, n = w_mod.shape
    return pl.pallas_call(
        _mod_kernel,
        grid=(depth, n // tn),
        in_specs=[pl.BlockSpec((MOD_ROWS, d), lambda l, j: (0, 0)),
                  pl.BlockSpec((1, d, tn), lambda l, j: (l, 0, j)),
                  pl.BlockSpec((1, 1, tn), lambda l, j: (l, 0, j))],
        out_specs=pl.BlockSpec((1, MOD_ROWS, tn), lambda l, j: (l, 0, j)),
        out_shape=jax.ShapeDtypeStruct((depth, MOD_ROWS, n), F32),
        compiler_params=_cparams("parallel", "parallel"),
        name="mod_all",
    )(cc, w_mod, b_mod.reshape(depth, 1, n))


def _fold_kernel(cs_ref, w_ref, o_ref):
    o_ref[0] = jnp.dot(cs_ref[...], w_ref[0], preferred_element_type=F32,
                       precision=lax.Precision.HIGHEST)


def _fourier_fold(w_fourier):
    depth, groups, c, _ = w_fourier.shape
    k = np.arange(c)
    ang = 2.0 * np.pi * np.outer(k, k) / c
    cs = np.concatenate([np.cos(ang), -np.sin(ang)], axis=0) / math.sqrt(c)
    out = pl.pallas_call(
        _fold_kernel,
        grid=(depth * groups,),
        in_specs=[pl.BlockSpec((2 * c, c), lambda i: (0, 0)),
                  pl.BlockSpec((1, c, c), lambda i: (i, 0, 0))],
        out_specs=pl.BlockSpec((1, 2 * c, c), lambda i: (i, 0, 0)),
        out_shape=jax.ShapeDtypeStruct((depth * groups, 2 * c, c), F32),
        compiler_params=_cparams("parallel"),
        name="fourier_fold",
    )(jnp.asarray(cs, F32), w_fourier.reshape(depth * groups, c, c))
    out = out.reshape(depth, groups, 2, c, c).transpose(0, 1, 3, 2, 4).reshape(depth, groups, c, 2 * c)
    return out.astype(BF16)


def _ln_mod_kernel(x_ref, c_ref, sh_ref, sc_ref, xa_ref, h_ref, *, n_lat_tiles):
    latent = pl.program_id(0) < n_lat_tiles

    @pl.when(latent)
    def _():
        xa_ref[...] = x_ref[...]

    @pl.when(jnp.logical_not(latent))
    def _():
        xa_ref[...] = c_ref[...]

    h_ref[...] = _modulate(xa_ref[...], sh_ref[0], sc_ref[0]).astype(h_ref.dtype)


def _mod_spec(d, chunk, row_fn):
    return pl.BlockSpec((1, 1, d), lambda i, *_: (row_fn(i), 0, chunk))


def _ln_mod(x_lat, x_ctx, mod3, row_fn, tm):
    n_lat, d = x_lat.shape
    n_ctx = x_ctx.shape[0]
    n_lat_tiles, n = n_lat // tm, n_lat + n_ctx
    return pl.pallas_call(
        functools.partial(_ln_mod_kernel, n_lat_tiles=n_lat_tiles),
        grid=(n // tm,),
        in_specs=[pl.BlockSpec((tm, d), lambda i: (jnp.minimum(i, n_lat_tiles - 1), 0)),
                  pl.BlockSpec((tm, d), lambda i: (jnp.maximum(i - n_lat_tiles, 0), 0)),
                  _mod_spec(d, 0, row_fn), _mod_spec(d, 1, row_fn)],
        out_specs=[pl.BlockSpec((tm, d), lambda i: (i, 0)), pl.BlockSpec((tm, d), lambda i: (i, 0))],
        out_shape=[jax.ShapeDtypeStruct((n, d), F32), jax.ShapeDtypeStruct((n, d), BF16)],
        compiler_params=_cparams("arbitrary"),
        name="ln_mod",
    )(x_lat, x_ctx, mod3, mod3)


def _rope(t, cos, s1, s2):
    return t * cos + pltpu.roll(t, HEAD_DIM - 32, 1) * s1 + pltpu.roll(t, 32, 1) * s2


def _rms(t, gain):
    return t * lax.rsqrt(jnp.mean(t * t, axis=-1, keepdims=True) + EPS) * gain


def _win_kernel(h_ref, w_ref, qg_ref, kg_ref, cos_ref, s1_ref, s2_ref, p_ref, vt_ref, *,
                kinds, tm, tn, scale):
    h = h_ref[...]
    cos, s1, s2 = cos_ref[...], s1_ref[...], s2_ref[...]
    per = tn // HEAD_DIM
    n_v = 0
    for b in range(len(kinds) // per):
        acc = _dot(h, w_ref[:, b * tn:(b + 1) * tn])
        for c in range(per):
            kind = kinds[b * per + c]
            col = (b * per + c) * HEAD_DIM
            t = acc[:, c * HEAD_DIM:(c + 1) * HEAD_DIM]
            if kind == "qa":
                t = _rope(_rms(t, qg_ref[...]), cos, s1, s2) * scale
            elif kind == "qb":
                t = _rope(t, cos, s1, s2) * scale
            elif kind == "ka":
                t = _rope(_rms(t, kg_ref[...]), cos, s1, s2)
            elif kind == "kb":
                t = _rope(t, cos, s1, s2)
            p_ref[:, col:col + HEAD_DIM] = t.astype(p_ref.dtype)
            if kind == "v":
                for r in range(tm // HEAD_DIM):
                    tile = t[r * HEAD_DIM:(r + 1) * HEAD_DIM]
                    vt_ref[r, n_v * HEAD_DIM:(n_v + 1) * HEAD_DIM, :] = tile.T.astype(vt_ref.dtype)
                n_v += 1


def _in_proj(h, w, qg, kg, cos, s1, s2, *, kinds, rope_fn, tm, tn, scale):
    n, d = h.shape
    in_w = w.shape[1]
    n_v = kinds.count("v")
    kern = functools.partial(_win_kernel, kinds=kinds, tm=tm, tn=tn, scale=scale)
    tab = pl.BlockSpec((tm, HEAD_DIM), lambda i: (rope_fn(i), 0))
    vec = pl.BlockSpec((1, HEAD_DIM), lambda i: (0, 0))
    return pl.pallas_call(
        kern,
        grid=(n // tm,),
        in_specs=[pl.BlockSpec((tm, d), lambda i: (i, 0)),
                  pl.BlockSpec((d, in_w), lambda i: (0, 0)),
                  vec, vec, tab, tab, tab],
        out_specs=[pl.BlockSpec((tm, in_w), lambda i: (i, 0)),
                   pl.BlockSpec((tm // HEAD_DIM, n_v * HEAD_DIM, HEAD_DIM), lambda i: (i, 0, 0))],
        out_shape=[jax.ShapeDtypeStruct((n, in_w), BF16),
                   jax.ShapeDtypeStruct((n // HEAD_DIM, n_v * HEAD_DIM, HEAD_DIM), BF16)],
        compiler_params=_cparams("parallel"),
        name="in_proj",
    )(h, w, qg, kg, cos, s1, s2)


def _scores_t(k, q):
    return lax.dot_general(k, q, (((1,), (1,)), ((), ())), preferred_element_type=F32)


def _vt_tile(vt_ref, first, count):
    tiles = [vt_ref[first + t] for t in range(count)]
    return tiles[0] if count == 1 else jnp.concatenate(tiles, axis=1)


def _online_first(s, vt):
    m = jnp.max(s, axis=0, keepdims=True)
    p = jnp.exp2(s - m)
    return m, jnp.sum(p, axis=0, keepdims=True), _dot(vt, p.astype(BF16))


def _online_step(s, vt, m, l, acc):
    m_new = jnp.maximum(m, jnp.max(s, axis=0, keepdims=True))
    p = jnp.exp2(s - m_new)
    alpha = jnp.exp2(m - m_new)
    return (m_new, alpha * l + jnp.sum(p, axis=0, keepdims=True),
            alpha * acc + _dot(vt, p.astype(BF16)))


def _global_attn_kernel(*refs, group, tk, n_chunks, ctx_len):
    if n_chunks:
        q_ref, k_ref, vt_ref, kc_ref, vtc_ref, o_ref, s_ref = refs
    else:
        q_ref, kc_ref, vtc_ref, o_ref = refs
    sub = tk // HEAD_DIM
    tq = q_ref.shape[0]
    cols = [slice(g * HEAD_DIM, (g + 1) * HEAD_DIM) for g in range(group)]
    qs = [q_ref[:, c] for c in cols]
    kc, vtc = kc_ref[...], _vt_tile(vtc_ref, 0, ctx_len // HEAD_DIM)
    if not n_chunks:
        carry = tuple(_online_first(_scores_t(kc, q), vtc) for q in qs)
    else:
        assert n_chunks % 2 == 0

        def qk(c, slot):
            k = k_ref[pl.ds(pl.multiple_of(c * tk, tk), tk), :]
            for g, q in enumerate(qs):
                s_ref[slot, g] = _scores_t(k, q)

        def softmax_pv(c, slot, carry):
            vt = _vt_tile(vt_ref, c * sub, sub)
            return tuple(_online_step(s_ref[slot, g], vt, *st) for g, st in enumerate(carry))

        def body(i, carry):
            qk(2 * i + 1, 1)
            carry = softmax_pv(2 * i, 0, carry)
            qk(2 * i + 2, 0)
            return softmax_pv(2 * i + 1, 1, carry)

        init = (jnp.full((1, tq), -jnp.inf, F32), jnp.zeros((1, tq), F32), jnp.zeros((HEAD_DIM, tq), F32))
        qk(0, 0)
        carry = lax.fori_loop(0, n_chunks // 2 - 1, body, (init,) * group)
        qk(n_chunks - 1, 1)
        carry = softmax_pv(n_chunks - 2, 0, carry)
        ctx_scores = [_scores_t(kc, q) for q in qs]
        carry = softmax_pv(n_chunks - 1, 1, carry)
        carry = tuple(_online_step(s, vtc, *st) for s, st in zip(ctx_scores, carry))
    for c, (_, l, acc) in zip(cols, carry):
        o_ref[:, c] = (acc / l).T.astype(o_ref.dtype)


def _window_attn_kernel(*refs, group, tq, seq, ctx_len, local):
    if local:
        sink_ref, q_ref, k_ref, vt_ref, kc_ref, vtc_ref, o_ref = refs
    else:
        sink_ref, q_ref, kc_ref, vtc_ref, o_ref = refs
    kv = pl.program_id(1)
    qi = pl.program_id(2)
    sub = tq // HEAD_DIM
    n_sub = seq // HEAD_DIM
    kc, vtc = kc_ref[...], _vt_tile(vtc_ref, 0, ctx_len // HEAD_DIM)
    if local:
        first = qi * sub
        prev = jnp.maximum(first - 1, 0)
        nxt = jnp.minimum(first + sub, n_sub - 1)
        k_loc = jnp.concatenate(
            [k_ref[pl.ds(pl.multiple_of(prev * HEAD_DIM, HEAD_DIM), HEAD_DIM), :],
             k_ref[pl.ds(pl.multiple_of(first * HEAD_DIM, tq), tq), :],
             k_ref[pl.ds(pl.multiple_of(nxt * HEAD_DIM, HEAD_DIM), HEAD_DIM), :]], axis=0)
        vt_loc = jnp.concatenate(
            [vt_ref[prev]] + [vt_ref[first + t] for t in range(sub)] + [vt_ref[nxt]], axis=1)
        shape = (tq + 2 * HEAD_DIM, tq)
        kpos = (first - 1) * HEAD_DIM + lax.broadcasted_iota(jnp.int32, shape, 0)
        qpos = qi * tq + lax.broadcasted_iota(jnp.int32, shape, 1)
        kpos = jnp.where(kpos < 0, -seq, jnp.where(kpos >= seq, 2 * seq, kpos))
        bias = jnp.where(jnp.abs(qpos - kpos) <= WINDOW, 0.0, NEG_INF)

    for g in range(group):
        cols = slice(g * HEAD_DIM, (g + 1) * HEAD_DIM)
        q = q_ref[:, cols]
        sink = sink_ref[kv * group + g] * LOG2_E
        s_c = _scores_t(kc, q)
        m = jnp.maximum(jnp.max(s_c, axis=0, keepdims=True), sink)
        if local:
            s_l = _scores_t(k_loc, q) + bias
            m = jnp.maximum(m, jnp.max(s_l, axis=0, keepdims=True))
        p_c = jnp.exp2(s_c - m)
        denom = jnp.sum(p_c, axis=0, keepdims=True) + jnp.exp2(sink - m)
        acc = _dot(vtc, p_c.astype(BF16))
        if local:
            p_l = jnp.exp2(s_l - m)
            denom = denom + jnp.sum(p_l, axis=0, keepdims=True)
            acc = acc + _dot(vt_loc, p_l.astype(BF16))
        o_ref[:, cols] = (acc / denom).T.astype(o_ref.dtype)


def _attention(p, vt3, out_prev, sink, *, window, latent_queries, batch, seq, ctx_len, q_off, k_off,
               v_row, group, out_width, tq, tk):
    n_all = p.shape[0]
    gw = group * HEAD_DIM
    ctx_blk0 = batch * seq // ctx_len
    kc_spec = pl.BlockSpec((ctx_len, HEAD_DIM), lambda b, kv, qi, *_: (ctx_blk0 + b, k_off // HEAD_DIM + kv))
    vtc_spec = pl.BlockSpec((ctx_len // HEAD_DIM, HEAD_DIM, HEAD_DIM),
                            lambda b, kv, qi, *_: (ctx_blk0 + b, v_row + kv, 0))
    if latent_queries:
        n_q = seq // tq
        q_blk = lambda b, kv, qi, *_: (b * n_q + qi, q_off // gw + kv)
        lat_specs = [pl.BlockSpec((seq, HEAD_DIM), lambda b, kv, qi, *_: (b, k_off // HEAD_DIM + kv)),
                     pl.BlockSpec((seq // HEAD_DIM, HEAD_DIM, HEAD_DIM), lambda b, kv, qi, *_: (b, v_row + kv, 0))]
        lat_args = [p, vt3]
    else:
        tq = ctx_len
        n_q = 1
        q_blk = lambda b, kv, qi, *_: (ctx_blk0 + b, q_off // gw + kv)
        lat_specs, lat_args = [], []
    q_spec = pl.BlockSpec((tq, gw), q_blk)
    o_spec = pl.BlockSpec((tq, gw), lambda b, kv, qi, *_: (q_blk(b, kv, qi)[0], kv))
    if window:
        kern = functools.partial(_window_attn_kernel, group=group, tq=tq, seq=seq, ctx_len=ctx_len,
                                 local=latent_queries)
        pre_specs = [pl.BlockSpec(memory_space=pltpu.SMEM)]
        pre_args = [sink]
    else:
        kern = functools.partial(_global_attn_kernel, group=group, tk=tk,
                                 n_chunks=seq // tk if latent_queries else 0, ctx_len=ctx_len)
        pre_specs, pre_args = [], []
    scratch = [pltpu.VMEM((2, group, tk, tq), F32)] if latent_queries and not window else []
    in_specs = pre_specs + [q_spec] + lat_specs + [kc_spec, vtc_spec]
    args = pre_args + [p] + lat_args + [p, vt3]
    aliases = {}
    if out_prev is not None:
        in_specs.append(pl.BlockSpec(memory_space=pl.ANY))
        args.append(out_prev)
        aliases = {len(args) - 1: 0}
        body = kern

        def kern(*refs):
            return body(*refs[:-2], refs[-1])
    return pl.pallas_call(
        kern,
        grid=(batch, A_KV_HEADS, n_q),
        in_specs=in_specs,
        out_specs=o_spec,
        out_shape=jax.ShapeDtypeStruct((n_all, out_width), BF16),
        input_output_aliases=aliases,
        scratch_shapes=scratch,
        compiler_params=_cparams("parallel", "parallel", "arbitrary"),
        name=("window" if window else "global") + ("_lat" if latent_queries else "_ctx"),
    )(*args)


def _chan_kernel(u_ref, w_ref, v_ref):
    r = _dot(u_ref[...], w_ref[0])
    c = r.shape[1] // 2
    v_ref[0] = r[:, :c].astype(v_ref.dtype)
    v_ref[1] = r[:, c:].astype(v_ref.dtype)


def _fourier_channels(p, wfold_l, f_off, tm):
    n = p.shape[0]
    groups, c, _ = wfold_l.shape
    return pl.pallas_call(
        _chan_kernel,
        grid=(n // tm, groups),
        in_specs=[pl.BlockSpec((tm, c), lambda i, g: (i, f_off // c + g)),
                  pl.BlockSpec((1, c, 2 * c), lambda i, g: (g, 0, 0))],
        out_specs=pl.BlockSpec((2, tm, c), lambda i, g: (0, i, g)),
        out_shape=jax.ShapeDtypeStruct((2, n, groups * c), BF16),
        compiler_params=_cparams("parallel", "parallel"),
        name="fourier_channels",
    )(p, wfold_l)


def _dft1_kernel(w_ref, v_ref, y_ref):
    x = jnp.concatenate([v_ref[0], v_ref[1]], axis=0)
    y = _dot(w_ref[...], x)
    n1 = y.shape[0] // 2
    y_ref[0] = y[:n1].astype(y_ref.dtype)
    y_ref[1] = y[n1:].astype(y_ref.dtype)


def _dft2_kernel(m_ref, y_ref, o_ref, *, kb, cols):
    for kk in range(kb):
        x = jnp.concatenate([y_ref[0, kk], y_ref[1, kk]], axis=0)
        o_ref[:, kk * cols:(kk + 1) * cols] = _dot(m_ref[kk], x).astype(o_ref.dtype)


def _dense_dft_kernel(m_ref, v_ref, prev_ref, o_ref):
    del prev_ref
    x = jnp.concatenate([v_ref[0], v_ref[1]], axis=0)
    o_ref[...] = _dot(m_ref[...], x).astype(o_ref.dtype)


def _dft_tables(seq, ctx_len):
    n1 = n2 = int(round(math.sqrt(seq)))
    assert n1 * n2 == seq
    a1 = 2.0 * np.pi * np.outer(np.arange(n1), np.arange(n1)) / n1
    c1, s1 = np.cos(a1), np.sin(a1)
    w1 = np.block([[c1, s1], [-s1, c1]])
    k = np.arange(n1)[:, None, None] + n1 * np.arange(n2)[None, :, None]
    a2 = 2.0 * np.pi * ((k * np.arange(n2)[None, None, :]) % seq) / seq
    m2 = np.concatenate([np.cos(a2), np.sin(a2)], axis=2) / math.sqrt(seq)
    ac = 2.0 * np.pi * np.outer(np.arange(ctx_len), np.arange(ctx_len)) / ctx_len
    mc = np.concatenate([np.cos(ac), np.sin(ac)], axis=1) / math.sqrt(ctx_len)
    return n1, n2, jnp.asarray(w1, BF16), jnp.asarray(m2, BF16), jnp.asarray(mc, BF16)


def _fourier_positions(v, tables, *, batch, seq, ctx_len, with_ctx, tn=4096, kb=8):
    n1, n2, w1, m2, mc = tables
    _, n_all, cols = v.shape
    lanes = n2 * cols
    tn = min(tn, lanes)
    kb = min(kb, n1)
    v2 = v.reshape(2, n_all // n2, lanes)
    y = pl.pallas_call(
        _dft1_kernel,
        grid=(batch, lanes // tn),
        in_specs=[pl.BlockSpec((2 * n1, 2 * n1), lambda b, j: (0, 0)),
                  pl.BlockSpec((2, n1, tn), lambda b, j: (0, b, j))],
        out_specs=pl.BlockSpec((2, n1, tn), lambda b, j: (0, b, j)),
        out_shape=jax.ShapeDtypeStruct((2, batch * n1, lanes), BF16),
        compiler_params=_cparams("parallel", "parallel"),
        name="dft_stage1",
    )(w1, v2)
    y4 = y.reshape(2, batch * n1, n2, cols)
    groups = n1 // kb
    o = pl.pallas_call(
        functools.partial(_dft2_kernel, kb=kb, cols=cols),
        grid=(batch, groups),
        in_specs=[pl.BlockSpec((kb, n2, 2 * n2), lambda b, kg: (kg, 0, 0)),
                  pl.BlockSpec((2, kb, n2, cols), lambda b, kg: (0, b * groups + kg, 0, 0))],
        out_specs=pl.BlockSpec((n2, kb * cols), lambda b, kg: (b, kg)),
        out_shape=jax.ShapeDtypeStruct((n_all // n1, n1 * cols), BF16),
        compiler_params=_cparams("parallel", "parallel"),
        name="dft_stage2",
    )(m2, y4)
    o = o.reshape(n_all, cols)
    if not with_ctx:
        return o
    blk0 = batch * seq // ctx_len
    return pl.pallas_call(
        _dense_dft_kernel,
        grid=(batch,),
        in_specs=[pl.BlockSpec((ctx_len, 2 * ctx_len), lambda b: (0, 0)),
                  pl.BlockSpec((2, ctx_len, cols), lambda b: (0, blk0 + b, 0)),
                  pl.BlockSpec(memory_space=pl.ANY)],
        out_specs=pl.BlockSpec((ctx_len, cols), lambda b: (blk0 + b, 0)),
        out_shape=jax.ShapeDtypeStruct((n_all, cols), BF16),
        input_output_aliases={2: 0},
        compiler_params=_cparams("parallel"),
        name="dft_ctx",
    )(mc, v, o)


def _out_proj_kernel(oa_ref, ob_ref, of_ref, wa_ref, wb_ref, wf_ref, x_ref, g_ref, lg_ref, lb_ref,
                     sh_ref, sc_ref, xo_ref, ho_ref, *, alpha, row_chunk):
    for r in range(oa_ref.shape[0] // row_chunk):
        rows = slice(r * row_chunk, (r + 1) * row_chunk)
        y = (_dot(oa_ref[rows, :], wa_ref[...]) + _dot(ob_ref[rows, :], wb_ref[...])
             + _dot(of_ref[rows, :], wf_ref[...]))
        xn = _post_norm(x_ref[rows, :], y, g_ref[0], lg_ref[...], lb_ref[...], alpha)
        xo_ref[rows, :] = xn
        ho_ref[rows, :] = _modulate(xn, sh_ref[0], sc_ref[0]).astype(ho_ref.dtype)


def _out_proj(oa, ob, of, w_out, x_all, mod3, lg, lb, *, row_fn, n_rows, tm, alpha):
    n_all, d = x_all.shape
    wa, wb, wf = oa.shape[1], ob.shape[1], of.shape[1]
    assert wa % wb == 0 and wb == wf
    row = lambda i: (i, 0)
    vec = pl.BlockSpec((1, d), lambda i: (0, 0))
    return pl.pallas_call(
        functools.partial(_out_proj_kernel, alpha=alpha, row_chunk=min(tm, 256)),
        grid=(n_rows // tm,),
        in_specs=[pl.BlockSpec((tm, wa), row), pl.BlockSpec((tm, wb), row), pl.BlockSpec((tm, wf), row),
                  pl.BlockSpec((wa, d), lambda i: (0, 0)),
                  pl.BlockSpec((wb, d), lambda i: (wa // wb, 0)),
                  pl.BlockSpec((wf, d), lambda i: (wa // wb + 1, 0)),
                  pl.BlockSpec((tm, d), row),
                  _mod_spec(d, 2, row_fn), vec, vec, _mod_spec(d, 3, row_fn), _mod_spec(d, 4, row_fn)],
        out_specs=[pl.BlockSpec((tm, d), row), pl.BlockSpec((tm, d), row)],
        out_shape=[jax.ShapeDtypeStruct((n_all, d), F32), jax.ShapeDtypeStruct((n_all, d), BF16)],
        compiler_params=_cparams("parallel"),
        name="out_proj",
    )(oa, ob, of, w_out, w_out, w_out, x_all, mod3, lg, lb, mod3, mod3)


HALO = 16


def _ffn_kernel(hp_ref, hm_ref, hn_ref, wu_ref, wg_ref, cw_ref, cb_ref, wd_ref, x_ref, g_ref, lg_ref,
                lb_ref, *rest, tm, n_lat_tiles, seq, ctx_len, alpha, emit_h, n_split=2):
    if emit_h:
        sh_ref, sc_ref, xo_ref, ho_ref, hext_ref, acc_ref = rest
    else:
        xo_ref, hext_ref, acc_ref = rest
    i = pl.program_id(0)
    j = pl.program_id(1)

    @pl.when(j == 0)
    def _():
        hext_ref[0:HALO] = hp_ref[...]
        hext_ref[HALO:HALO + tm] = hm_ref[...]
        hext_ref[HALO + tm:] = hn_ref[...]
        acc_ref[...] = jnp.zeros_like(acc_ref)

    tf = wu_ref.shape[1]
    th = tf // n_split
    period = jnp.where(i < n_lat_tiles, seq, ctx_len)
    pos = (i * tm + lax.broadcasted_iota(jnp.int32, (tm, th), 0)) & (period - 1)
    first_row, last_row = pos == 0, pos == period - 1
    h_ext, h_main = hext_ref[...], hm_ref[...]
    part = None
    for s in range(n_split):
        sl = slice(s * th, (s + 1) * th)
        gext = _dot(h_ext, wg_ref[:, sl])
        u = _dot(h_main, wu_ref[:, sl])
        g_prev = jnp.where(first_row, 0.0, gext[HALO - 1:HALO - 1 + tm])
        g_cur = gext[HALO:HALO + tm]
        g_next = jnp.where(last_row, 0.0, gext[HALO + 1:HALO + 1 + tm])
        cw = cw_ref[:, sl]
        gc = cb_ref[:, sl] + cw[0:1] * g_prev + cw[1:2] * g_cur + cw[2:3] * g_next
        act = (gc * jax.nn.sigmoid(gc) * u).astype(BF16)
        d_part = _dot(act, wd_ref[sl, :])
        part = d_part if part is None else part + d_part
    acc_ref[...] += part

    @pl.when(j == pl.num_programs(1) - 1)
    def _():
        xn = _post_norm(x_ref[...], acc_ref[...], g_ref[0], lg_ref[...], lb_ref[...], alpha)
        xo_ref[...] = xn
        if emit_h:
            ho_ref[...] = _modulate(xn, sh_ref[0], sc_ref[0]).astype(ho_ref.dtype)


def _conv_ffn(h, w_up, w_gate, conv_w, conv_b, w_down, x_all, mod3, lg, lb, mod3_next, *, row_fn,
              n_rows, n_out_rows, tm, tf, seq, ctx_len, n_lat_tiles, alpha):
    n_all, d = x_all.shape
    d_ff = w_up.shape[1]
    emit_h = mod3_next is not None
    per = tm // HALO
    last_halo = n_rows // HALO - 1
    row = lambda i, j: (i, 0)
    vec = pl.BlockSpec((1, d), lambda i, j: (0, 0))
    in_specs = [pl.BlockSpec((HALO, d), lambda i, j: (jnp.maximum(i * per - 1, 0), 0)),
                pl.BlockSpec((tm, d), row),
                pl.BlockSpec((HALO, d), lambda i, j: (jnp.minimum((i + 1) * per, last_halo), 0)),
                pl.BlockSpec((d, tf), lambda i, j: (0, j)),
                pl.BlockSpec((d, tf), lambda i, j: (0, j)),
                pl.BlockSpec((conv_w.shape[0], tf), lambda i, j: (0, j)),
                pl.BlockSpec((1, tf), lambda i, j: (0, j)),
                pl.BlockSpec((tf, d), lambda i, j: (j, 0)),
                pl.BlockSpec((tm, d), row),
                _mod_spec(d, 5, row_fn), vec, vec]
    args = [h, h, h, w_up, w_gate, conv_w, conv_b, w_down, x_all, mod3, lg, lb]
    out_specs = [pl.BlockSpec((tm, d), row)]
    out_shape = [jax.ShapeDtypeStruct((n_out_rows, d), F32)]
    if emit_h:
        in_specs += [_mod_spec(d, 0, row_fn), _mod_spec(d, 1, row_fn)]
        args += [mod3_next, mod3_next]
        out_specs.append(pl.BlockSpec((tm, d), row))
        out_shape.append(jax.ShapeDtypeStruct((n_out_rows, d), BF16))
    kern = functools.partial(_ffn_kernel, tm=tm, n_lat_tiles=n_lat_tiles, seq=seq, ctx_len=ctx_len,
                             alpha=alpha, emit_h=emit_h)
    return pl.pallas_call(
        kern,
        grid=(n_rows // tm, d_ff // tf),
        in_specs=in_specs,
        out_specs=out_specs,
        out_shape=out_shape,
        scratch_shapes=[pltpu.VMEM((tm + 2 * HALO, d), BF16), pltpu.VMEM((tm, d), F32)],
        compiler_params=_cparams("parallel", "arbitrary"),
        name="conv_ffn",
    )(*args)


def _rope_tables(seq, pad_rows):
    rows = seq // GRID_W
    row = jnp.repeat(jnp.arange(rows), GRID_W).astype(F32)
    col = jnp.tile(jnp.arange(GRID_W), rows).astype(F32)
    n_freq = HEAD_DIM // 4
    inv_freq = ROPE_BASE ** (-jnp.arange(n_freq, dtype=F32) / n_freq)
    ar, ac = row[:, None] * inv_freq, col[:, None] * inv_freq
    zero = jnp.zeros_like(ar)
    cos = jnp.concatenate([jnp.cos(ar), jnp.cos(ar), jnp.cos(ac), jnp.cos(ac)], axis=1)
    s1 = jnp.concatenate([-jnp.sin(ar), zero, -jnp.sin(ac), zero], axis=1)
    s2 = jnp.concatenate([zero, jnp.sin(ar), zero, jnp.sin(ac)], axis=1)
    pad = lambda t, v: jnp.concatenate([t, jnp.full((pad_rows, HEAD_DIM), v, F32)], axis=0)
    return pad(cos, 1.0), pad(s1, 0.0), pad(s2, 0.0)


def _forward(x, c, ctx, c_ctx, w_mod, b_mod, w_in, q_gain_a, k_gain_a, sink_b, w_fourier, w_out,
             ln1_g, ln1_b, w_up, w_gate, conv_w, conv_b, w_down, ln2_g, ln2_b,
             *, tm=512, tm_in=512, tn_in=512, tf=512, tq=512, tk=512):
    batch, seq, d = x.shape
    ctx_len = ctx.shape[1]
    depth = w_mod.shape[0]
    n_lat, n_ctx = batch * seq, batch * ctx_len
    n_all = n_lat + n_ctx
    qa_w, qb_w, f_w = d // 2, d // 4, d // 4
    ka_w, kb_w = A_KV_HEADS * HEAD_DIM, B_KV_HEADS * HEAD_DIM
    assert f_w == F_GROUPS * HEAD_DIM and batch + 1 <= MOD_ROWS
    assert seq & (seq - 1) == 0 and ctx_len & (ctx_len - 1) == 0
    widths = (("qa", qa_w), ("qb", qb_w), ("ka", ka_w), ("v", ka_w), ("kb", kb_w), ("v", kb_w), ("f", f_w))
    kinds = tuple(kind for kind, w in widths for _ in range(w // HEAD_DIM))
    qb_off = qa_w
    ka_off = qa_w + qb_w
    kb_off = ka_off + 2 * ka_w
    f_off = kb_off + 2 * kb_w
    alpha = (2.0 * depth) ** 0.25
    scale = HEAD_DIM ** -0.5 * LOG2_E

    def row_fn_for(t):
        return lambda i: jnp.where(i < n_lat // t, i // (seq // t), batch)

    def rope_fn(i):
        return jnp.where(i < n_lat // tm_in, i % (seq // tm_in), seq // tm_in)

    cc = jnp.concatenate([c, c_ctx[None], jnp.zeros((MOD_ROWS - batch - 1, d), F32)], axis=0)
    mod = _mod_all(cc, w_mod, b_mod)
    wfold = _fourier_fold(w_fourier)
    cos, s1, s2 = _rope_tables(seq, tm_in)
    tables = _dft_tables(seq, ctx_len)
    mod3 = [mod[l].reshape(MOD_ROWS, 1, 6 * d) for l in range(depth)]
    vec = lambda t: t.reshape(1, -1)

    x_cur, h = _ln_mod(x.reshape(n_lat, d), ctx.reshape(n_ctx, d), mod3[0], row_fn_for(tm), tm)
    for l in range(depth):
        last = l == depth - 1
        w_in_b, w_out_b = w_in[l].astype(BF16), w_out[l].astype(BF16)
        w_up_b, w_gate_b, w_down_b = w_up[l].astype(BF16), w_gate[l].astype(BF16), w_down[l].astype(BF16)
        p, vt3 = _in_proj(h, w_in_b, vec(q_gain_a[l]), vec(k_gain_a[l]), cos, s1, s2, kinds=kinds,
                          rope_fn=rope_fn, tm=tm_in, tn=tn_in, scale=scale)
        common = dict(batch=batch, seq=seq, ctx_len=ctx_len, tq=tq, tk=tk)
        ga = dict(window=False, q_off=0, k_off=ka_off, v_row=0, group=qa_w // ka_w, out_width=qa_w, **common)
        gb = dict(window=True, q_off=qb_off, k_off=kb_off, v_row=A_KV_HEADS, group=qb_w // kb_w,
                  out_width=qb_w, **common)
        o_a = _attention(p, vt3, None, None, latent_queries=True, **ga)
        o_b = _attention(p, vt3, None, sink_b[l], latent_queries=True, **gb)
        v = _fourier_channels(p, wfold[l], f_off, tm)
        if not last:
            o_a = _attention(p, vt3, o_a, None, latent_queries=False, **ga)
            o_b = _attention(p, vt3, o_b, sink_b[l], latent_queries=False, **gb)
        o_f = _fourier_positions(v, tables, batch=batch, seq=seq, ctx_len=ctx_len, with_ctx=not last)
        n_rows = n_lat if last else n_all
        x_mid, h_mid = _out_proj(o_a, o_b, o_f, w_out_b, x_cur, mod3[l], vec(ln1_g[l]), vec(ln1_b[l]),
                                 row_fn=row_fn_for(tm), n_rows=n_rows, tm=tm, alpha=alpha)
        outs = _conv_ffn(h_mid, w_up_b, w_gate_b, conv_w[l], vec(conv_b[l]), w_down_b, x_mid,
                         mod3[l], vec(ln2_g[l]), vec(ln2_b[l]), None if last else mod3[l + 1],
                         row_fn=row_fn_for(tm), n_rows=n_rows, n_out_rows=n_rows, tm=tm, tf=tf, seq=seq,
                         ctx_len=ctx_len, n_lat_tiles=n_lat // tm, alpha=alpha)
        if last:
            x_cur = outs[0]
        else:
            x_cur, h = outs
    return x_cur.reshape(batch, seq, d)


def kernel(x, c, ctx, c_ctx, w_mod, b_mod, w_in, q_gain_a, k_gain_a, sink_b, w_fourier, w_out, ln1_g, ln1_b,
           w_up, w_gate, conv_w, conv_b, w_down, ln2_g, ln2_b):
    return _forward(x, c, ctx, c_ctx, w_mod, b_mod, w_in, q_gain_a, k_gain_a, sink_b, w_fourier, w_out,
                    ln1_g, ln1_b, w_up, w_gate, conv_w, conv_b, w_down, ln2_g, ln2_b)
```

```python
import functools
import math

import numpy as np
import jax
import jax.numpy as jnp
from jax import lax
from jax.experimental import pallas as pl
from jax.experimental.pallas import tpu as pltpu

F32 = jnp.float32
BF16 = jnp.bfloat16

HEAD_DIM = 128
A_KV_HEADS = 2
B_KV_HEADS = 2
F_GROUPS = 4
WINDOW = 128
GRID_W = 64
ROPE_BASE = 10000.0
EPS = 1e-6
NEG_INF = -1e30
LOG2_E = math.log2(math.e)
MOD_ROWS = 16
VMEM_LIMIT_BYTES = 56 * 1024 * 1024


def _cparams(*sem):
    return pltpu.CompilerParams(dimension_semantics=sem, vmem_limit_bytes=VMEM_LIMIT_BYTES)


def _dot(a, b):
    return jnp.dot(a, b, preferred_element_type=F32)


def _ln(v):
    mu = jnp.mean(v, axis=-1, keepdims=True)
    d = v - mu
    var = jnp.mean(d * d, axis=-1, keepdims=True)
    return d * lax.rsqrt(var + EPS)


def _post_norm(x, y, gate, lg, lb, alpha):
    return _ln(alpha * x + gate * y) * lg + lb


def _modulate(x, sh, sc):
    return _ln(x) * (1.0 + sc) + sh


def _mod_kernel(c_ref, w_ref, b_ref, o_ref):
    a = c_ref[...]
    a = a * jax.nn.sigmoid(a)
    o_ref[0] = jnp.dot(a, w_ref[0], preferred_element_type=F32,
                       precision=lax.Precision.HIGHEST) + b_ref[0]


def _mod_all(cc, w_mod, b_mod, tn=1024):
    depth, d, n = w_mod.shape
    return pl.pallas_call(
        _mod_kernel,
        grid=(depth, n // tn),
        in_specs=[pl.BlockSpec((MOD_ROWS, d), lambda l, j: (0, 0)),
                  pl.BlockSpec((1, d, tn), lambda l, j: (l, 0, j)),
                  pl.BlockSpec((1, 1, tn), lambda l, j: (l, 0, j))],
        out_specs=pl.BlockSpec((1, MOD_ROWS, tn), lambda l, j: (l, 0, j)),
        out_shape=jax.ShapeDtypeStruct((depth, MOD_ROWS, n), F32),
        compiler_params=_cparams("parallel", "parallel"),
        name="mod_all",
    )(cc, w_mod, b_mod.reshape(depth, 1, n))


def _fold_kernel(cs_ref, w_ref, o_ref):
    o_ref[0] = jnp.dot(cs_ref[...], w_ref[0], preferred_element_type=F32,
                       precision=lax.Precision.HIGHEST)


def _fourier_fold(w_fourier):
    depth, groups, c, _ = w_fourier.shape
    k = np.arange(c)
    ang = 2.0 * np.pi * np.outer(k, k) / c
    cs = np.concatenate([np.cos(ang), -np.sin(ang)], axis=0) / math.sqrt(c)
    out = pl.pallas_call(
        _fold_kernel,
        grid=(depth * groups,),
        in_specs=[pl.BlockSpec((2 * c, c), lambda i: (0, 0)),
                  pl.BlockSpec((1, c, c), lambda i: (i, 0, 0))],
        out_specs=pl.BlockSpec((1, 2 * c, c), lambda i: (i, 0, 0)),
        out_shape=jax.ShapeDtypeStruct((depth * groups, 2 * c, c), F32),
        compiler_params=_cparams("parallel"),
        name="fourier_fold",
    )(jnp.asarray(cs, F32), w_fourier.reshape(depth * groups, c, c))
    out = out.reshape(depth, groups, 2, c, c).transpose(0, 1, 3, 2, 4).reshape(depth, groups, c, 2 * c)
    return out.astype(BF16)


def _ln_mod_kernel(x_ref, c_ref, sh_ref, sc_ref, xa_ref, h_ref, *, n_lat_tiles):
    latent = pl.program_id(0) < n_lat_tiles

    @pl.when(latent)
    def _():
        xa_ref[...] = x_ref[...]

    @pl.when(jnp.logical_not(latent))
    def _():
        xa_ref[...] = c_ref[...]

    h_ref[...] = _modulate(xa_ref[...], sh_ref[0], sc_ref[0]).astype(h_ref.dtype)


def _mod_spec(d, chunk, row_fn):
    return pl.BlockSpec((1, 1, d), lambda i, *_: (row_fn(i), 0, chunk))


def _ln_mod(x_lat, x_ctx, mod3, row_fn, tm):
    n_lat, d = x_lat.shape
    n_ctx = x_ctx.shape[0]
    n_lat_tiles, n = n_lat // tm, n_lat + n_ctx
    return pl.pallas_call(
        functools.partial(_ln_mod_kernel, n_lat_tiles=n_lat_tiles),
        grid=(n // tm,),
        in_specs=[pl.BlockSpec((tm, d), lambda i: (jnp.minimum(i, n_lat_tiles - 1), 0)),
                  pl.BlockSpec((tm, d), lambda i: (jnp.maximum(i - n_lat_tiles, 0), 0)),
                  _mod_spec(d, 0, row_fn), _mod_spec(d, 1, row_fn)],
        out_specs=[pl.BlockSpec((tm, d), lambda i: (i, 0)), pl.BlockSpec((tm, d), lambda i: (i, 0))],
        out_shape=[jax.ShapeDtypeStruct((n, d), F32), jax.ShapeDtypeStruct((n, d), BF16)],
        compiler_params=_cparams("arbitrary"),
        name="ln_mod",
    )(x_lat, x_ctx, mod3, mod3)


def _rope(t, cos, s1, s2):
    return t * cos + pltpu.roll(t, HEAD_DIM - 32, 1) * s1 + pltpu.roll(t, 32, 1) * s2


def _rms(t, gain):
    return t * lax.rsqrt(jnp.mean(t * t, axis=-1, keepdims=True) + EPS) * gain


def _win_kernel(h_ref, w_ref, qg_ref, kg_ref, cos_ref, s1_ref, s2_ref, p_ref, vt_ref, *,
                kinds, tm, tn, scale):
    h = h_ref[...]
    cos, s1, s2 = cos_ref[...], s1_ref[...], s2_ref[...]
    per = tn // HEAD_DIM
    n_v = 0
    for b in range(len(kinds) // per):
        acc = _dot(h, w_ref[:, b * tn:(b + 1) * tn])
        for c in range(per):
            kind = kinds[b * per + c]
            col = (b * per + c) * HEAD_DIM
            t = acc[:, c * HEAD_DIM:(c + 1) * HEAD_DIM]
            if kind == "qa":
                t = _rope(_rms(t, qg_ref[...]), cos, s1, s2) * scale
            elif kind == "qb":
                t = _rope(t, cos, s1, s2) * scale
            elif kind == "ka":
                t = _rope(_rms(t, kg_ref[...]), cos, s1, s2)
            elif kind == "kb":
                t = _rope(t, cos, s1, s2)
            p_ref[:, col:col + HEAD_DIM] = t.astype(p_ref.dtype)
            if kind == "v":
                for r in range(tm // HEAD_DIM):
                    tile = t[r * HEAD_DIM:(r + 1) * HEAD_DIM]
                    vt_ref[r, n_v * HEAD_DIM:(n_v + 1) * HEAD_DIM, :] = tile.T.astype(vt_ref.dtype)
                n_v += 1


def _in_proj(h, w, qg, kg, cos, s1, s2, *, kinds, rope_fn, tm, tn, scale):
    n, d = h.shape
    in_w = w.shape[1]
    n_v = kinds.count("v")
    kern = functools.partial(_win_kernel, kinds=kinds, tm=tm, tn=tn, scale=scale)
    tab = pl.BlockSpec((tm, HEAD_DIM), lambda i: (rope_fn(i), 0))
    vec = pl.BlockSpec((1, HEAD_DIM), lambda i: (0, 0))
    return pl.pallas_call(
        kern,
        grid=(n // tm,),
        in_specs=[pl.BlockSpec((tm, d), lambda i: (i, 0)),
                  pl.BlockSpec((d, in_w), lambda i: (0, 0)),
                  vec, vec, tab, tab, tab],
        out_specs=[pl.BlockSpec((tm, in_w), lambda i: (i, 0)),
                   pl.BlockSpec((tm // HEAD_DIM, n_v * HEAD_DIM, HEAD_DIM), lambda i: (i, 0, 0))],
        out_shape=[jax.ShapeDtypeStruct((n, in_w), BF16),
                   jax.ShapeDtypeStruct((n // HEAD_DIM, n_v * HEAD_DIM, HEAD_DIM), BF16)],
        compiler_params=_cparams("parallel"),
        name="in_proj",
    )(h, w, qg, kg, cos, s1, s2)


def _scores_t(k, q):
    return lax.dot_general(k, q, (((1,), (1,)), ((), ())), preferred_element_type=F32)


def _vt_tile(vt_ref, first, count):
    tiles = [vt_ref[first + t] for t in range(count)]
    return tiles[0] if count == 1 else jnp.concatenate(tiles, axis=1)


def _online_first(s, vt):
    m = jnp.max(s, axis=0, keepdims=True)
    p = jnp.exp2(s - m)
    return m, jnp.sum(p, axis=0, keepdims=True), _dot(vt, p.astype(BF16))


def _exact_zero(acc):
    bits = pltpu.bitcast(acc[0:1, 0:HEAD_DIM], jnp.uint32)
    bits = lax.shift_right_logical(lax.shift_right_logical(bits, jnp.uint32(16)), jnp.uint32(16))
    return pltpu.bitcast(bits, F32).astype(BF16)


SUM_ROWS = 16


def _with_ones(vt):
    return jnp.concatenate([vt, jnp.ones((SUM_ROWS, vt.shape[1]), vt.dtype)], axis=0)


def _online_step(s, vt_ones, m, acc):
    m_new = jnp.maximum(m, jnp.max(s, axis=0, keepdims=True))
    p = jnp.exp2(s - m_new).astype(BF16)
    return m_new, jnp.exp2(m - m_new) * acc + _dot(vt_ones, p)


def _global_attn_kernel(*refs, group, tk, n_chunks, ctx_len):
    if n_chunks:
        q_ref, k_ref, vt_ref, kc_ref, vtc_ref, o_ref = refs[:6]
        s_refs = refs[6:6 + 2 * group]
        acc_refs = refs[6 + 2 * group:]
    else:
        q_ref, kc_ref, vtc_ref, o_ref = refs
    sub = tk // HEAD_DIM
    tq = q_ref.shape[0]
    cols = [slice(g * HEAD_DIM, (g + 1) * HEAD_DIM) for g in range(group)]
    qs = [q_ref[:, c] for c in cols]
    kc, vtc = kc_ref[...], _vt_tile(vtc_ref, 0, ctx_len // HEAD_DIM)
    if not n_chunks:
        for c, q in zip(cols, qs):
            _, l, acc = _online_first(_scores_t(kc, q), vtc)
            o_ref[:, c] = (acc / l).T.astype(o_ref.dtype)
    else:
        assert n_chunks % 2 == 0

        def lat_keys(c):
            return k_ref[pl.ds(pl.multiple_of(c * tk, tk), tk), :]

        def lat_values(c):
            return _with_ones(_vt_tile(vt_ref, c * sub, sub))

        def stage(k_next, slot_next, vt_cur, slot_cur, ms):
            out, gate = [], None
            for g, q in enumerate(qs):
                if k_next is not None:
                    n_keys = k_next.shape[0]
                    s_refs[slot_next * group + g][0:n_keys, :] = _scores_t(
                        k_next if gate is None else k_next + gate, q)
                if vt_cur is not None:
                    n_keys = vt_cur.shape[1]
                    m_new, acc = _online_step(s_refs[slot_cur * group + g][0:n_keys, :], vt_cur, ms[g],
                                              acc_refs[g][...])
                    acc_refs[g][...] = acc
                    gate = _exact_zero(m_new)
                    out.append(m_new)
            return tuple(out)

        def body(i, ms):
            ms = stage(lat_keys(2 * i + 1), 1, lat_values(2 * i), 0, ms)
            return stage(lat_keys(2 * i + 2), 0, lat_values(2 * i + 1), 1, ms)

        for acc_ref in acc_refs:
            acc_ref[...] = jnp.zeros_like(acc_ref)
        ms = (jnp.full((1, tq), -jnp.inf, F32),) * group
        stage(lat_keys(0), 0, None, None, None)
        ms = lax.fori_loop(0, n_chunks // 2 - 1, body, ms)
        ms = stage(lat_keys(n_chunks - 1), 1, lat_values(n_chunks - 2), 0, ms)
        ms = stage(kc, 0, lat_values(n_chunks - 1), 1, ms)
        stage(None, None, _with_ones(vtc), 0, ms)
        for c, acc_ref in zip(cols, acc_refs):
            o = acc_ref[0:HEAD_DIM, :] / acc_ref[HEAD_DIM:HEAD_DIM + 1, :]
            o_ref[:, c] = o.T.astype(o_ref.dtype)


def _window_attn_kernel(*refs, group, tq, seq, ctx_len, local):
    if local:
        sink_ref, q_ref, k_ref, vt_ref, kc_ref, vtc_ref, o_ref = refs
    else:
        sink_ref, q_ref, kc_ref, vtc_ref, o_ref = refs
    kv = pl.program_id(1)
    qi = pl.program_id(2)
    sub = tq // HEAD_DIM
    n_sub = seq // HEAD_DIM
    kc, vtc = kc_ref[...], _vt_tile(vtc_ref, 0, ctx_len // HEAD_DIM)
    if local:
        first = qi * sub
        prev = jnp.maximum(first - 1, 0)
        nxt = jnp.minimum(first + sub, n_sub - 1)
        k_loc = jnp.concatenate(
            [k_ref[pl.ds(pl.multiple_of(prev * HEAD_DIM, HEAD_DIM), HEAD_DIM), :],
             k_ref[pl.ds(pl.multiple_of(first * HEAD_DIM, tq), tq), :],
             k_ref[pl.ds(pl.multiple_of(nxt * HEAD_DIM, HEAD_DIM), HEAD_DIM), :]], axis=0)
        vt_loc = jnp.concatenate(
            [vt_ref[prev]] + [vt_ref[first + t] for t in range(sub)] + [vt_ref[nxt]], axis=1)
        shape = (tq + 2 * HEAD_DIM, tq)
        kpos = (first - 1) * HEAD_DIM + lax.broadcasted_iota(jnp.int32, shape, 0)
        qpos = qi * tq + lax.broadcasted_iota(jnp.int32, shape, 1)
        kpos = jnp.where(kpos < 0, -seq, jnp.where(kpos >= seq, 2 * seq, kpos))
        bias = jnp.where(jnp.abs(qpos - kpos) <= WINDOW, 0.0, NEG_INF)

    for g in range(group):
        cols = slice(g * HEAD_DIM, (g + 1) * HEAD_DIM)
        q = q_ref[:, cols]
        sink = sink_ref[kv * group + g] * LOG2_E
        s_c = _scores_t(kc, q)
        m = jnp.maximum(jnp.max(s_c, axis=0, keepdims=True), sink)
        if local:
            s_l = _scores_t(k_loc, q) + bias
            m = jnp.maximum(m, jnp.max(s_l, axis=0, keepdims=True))
        p_c = jnp.exp2(s_c - m)
        denom = jnp.sum(p_c, axis=0, keepdims=True) + jnp.exp2(sink - m)
        acc = _dot(vtc, p_c.astype(BF16))
        if local:
            p_l = jnp.exp2(s_l - m)
            denom = denom + jnp.sum(p_l, axis=0, keepdims=True)
            acc = acc + _dot(vt_loc, p_l.astype(BF16))
        o_ref[:, cols] = (acc / denom).T.astype(o_ref.dtype)


def _attention(p, vt3, out_prev, sink, *, window, latent_queries, batch, seq, ctx_len, q_off, k_off,
               v_row, group, out_width, tq, tk):
    n_all = p.shape[0]
    gw = group * HEAD_DIM
    ctx_blk0 = batch * seq // ctx_len
    kc_spec = pl.BlockSpec((ctx_len, HEAD_DIM), lambda b, kv, qi, *_: (ctx_blk0 + b, k_off // HEAD_DIM + kv))
    vtc_spec = pl.BlockSpec((ctx_len // HEAD_DIM, HEAD_DIM, HEAD_DIM),
                            lambda b, kv, qi, *_: (ctx_blk0 + b, v_row + kv, 0))
    if latent_queries:
        n_q = seq // tq
        q_blk = lambda b, kv, qi, *_: (b * n_q + qi, q_off // gw + kv)
        lat_specs = [pl.BlockSpec((seq, HEAD_DIM), lambda b, kv, qi, *_: (b, k_off // HEAD_DIM + kv)),
                     pl.BlockSpec((seq // HEAD_DIM, HEAD_DIM, HEAD_DIM), lambda b, kv, qi, *_: (b, v_row + kv, 0))]
        lat_args = [p, vt3]
    else:
        tq = ctx_len
        n_q = 1
        q_blk = lambda b, kv, qi, *_: (ctx_blk0 + b, q_off // gw + kv)
        lat_specs, lat_args = [], []
    q_spec = pl.BlockSpec((tq, gw), q_blk)
    o_spec = pl.BlockSpec((tq, gw), lambda b, kv, qi, *_: (q_blk(b, kv, qi)[0], kv))
    if window:
        kern = functools.partial(_window_attn_kernel, group=group, tq=tq, seq=seq, ctx_len=ctx_len,
                                 local=latent_queries)
        pre_specs = [pl.BlockSpec(memory_space=pltpu.SMEM)]
        pre_args = [sink]
    else:
        kern = functools.partial(_global_attn_kernel, group=group, tk=tk,
                                 n_chunks=seq // tk if latent_queries else 0, ctx_len=ctx_len)
        pre_specs, pre_args = [], []
    scratch = []
    if latent_queries and not window:
        assert ctx_len <= tk
        scratch = [pltpu.VMEM((tk, tq), F32)] * (2 * group) + [pltpu.VMEM((HEAD_DIM + SUM_ROWS, tq), F32)] * group
    in_specs = pre_specs + [q_spec] + lat_specs + [kc_spec, vtc_spec]
    args = pre_args + [p] + lat_args + [p, vt3]
    aliases = {}
    if out_prev is not None:
        in_specs.append(pl.BlockSpec(memory_space=pl.ANY))
        args.append(out_prev)
        aliases = {len(args) - 1: 0}
        body = kern

        def kern(*refs):
            return body(*refs[:-2], refs[-1])
    return pl.pallas_call(
        kern,
        grid=(batch, A_KV_HEADS, n_q),
        in_specs=in_specs,
        out_specs=o_spec,
        out_shape=jax.ShapeDtypeStruct((n_all, out_width), BF16),
        input_output_aliases=aliases,
        scratch_shapes=scratch,
        compiler_params=_cparams("parallel", "parallel", "arbitrary"),
        name=("window" if window else "global") + ("_lat" if latent_queries else "_ctx"),
    )(*args)


def _chan_kernel(u_ref, w_ref, v_ref):
    groups, c, _ = w_ref.shape
    for g in range(groups):
        cols = slice(g * c, (g + 1) * c)
        r = _dot(u_ref[:, cols], w_ref[g])
        v_ref[0, :, cols] = r[:, :c].astype(v_ref.dtype)
        v_ref[1, :, cols] = r[:, c:].astype(v_ref.dtype)


def _fourier_channels(p, wfold_l, f_off, tm):
    n = p.shape[0]
    groups, c, _ = wfold_l.shape
    f_w = groups * c
    assert f_off % f_w == 0
    return pl.pallas_call(
        _chan_kernel,
        grid=(n // tm,),
        in_specs=[pl.BlockSpec((tm, f_w), lambda i: (i, f_off // f_w)),
                  pl.BlockSpec((groups, c, 2 * c), lambda i: (0, 0, 0))],
        out_specs=pl.BlockSpec((2, tm, f_w), lambda i: (0, i, 0)),
        out_shape=jax.ShapeDtypeStruct((2, n, f_w), BF16),
        compiler_params=_cparams("parallel"),
        name="fourier_channels",
    )(p, wfold_l)


def _dft1_kernel(w_ref, v_ref, y_ref):
    x = jnp.concatenate([v_ref[0], v_ref[1]], axis=0)
    y = _dot(w_ref[...], x)
    n1 = y.shape[0] // 2
    y_ref[0] = y[:n1].astype(y_ref.dtype)
    y_ref[1] = y[n1:].astype(y_ref.dtype)


def _dft2_kernel(m_ref, y_ref, o_ref, *, kb, cols):
    for kk in range(kb):
        x = jnp.concatenate([y_ref[0, kk], y_ref[1, kk]], axis=0)
        o_ref[:, kk * cols:(kk + 1) * cols] = _dot(m_ref[kk], x).astype(o_ref.dtype)


def _dense_dft_kernel(m_ref, v_ref, prev_ref, o_ref):
    del prev_ref
    x = jnp.concatenate([v_ref[0], v_ref[1]], axis=0)
    o_ref[...] = _dot(m_ref[...], x).astype(o_ref.dtype)


def _dft_tables(seq, ctx_len):
    n1 = n2 = int(round(math.sqrt(seq)))
    assert n1 * n2 == seq
    a1 = 2.0 * np.pi * np.outer(np.arange(n1), np.arange(n1)) / n1
    c1, s1 = np.cos(a1), np.sin(a1)
    w1 = np.block([[c1, s1], [-s1, c1]])
    k = np.arange(n1)[:, None, None] + n1 * np.arange(n2)[None, :, None]
    a2 = 2.0 * np.pi * ((k * np.arange(n2)[None, None, :]) % seq) / seq
    m2 = np.concatenate([np.cos(a2), np.sin(a2)], axis=2) / math.sqrt(seq)
    ac = 2.0 * np.pi * np.outer(np.arange(ctx_len), np.arange(ctx_len)) / ctx_len
    mc = np.concatenate([np.cos(ac), np.sin(ac)], axis=1) / math.sqrt(ctx_len)
    return n1, n2, jnp.asarray(w1, BF16), jnp.asarray(m2, BF16), jnp.asarray(mc, BF16)


def _fourier_positions(v, tables, *, batch, seq, ctx_len, with_ctx, tn=4096, kb=8):
    n1, n2, w1, m2, mc = tables
    _, n_all, cols = v.shape
    lanes = n2 * cols
    tn = min(tn, lanes)
    kb = min(kb, n1)
    v2 = v.reshape(2, n_all // n2, lanes)
    y = pl.pallas_call(
        _dft1_kernel,
        grid=(batch, lanes // tn),
        in_specs=[pl.BlockSpec((2 * n1, 2 * n1), lambda b, j: (0, 0)),
                  pl.BlockSpec((2, n1, tn), lambda b, j: (0, b, j))],
        out_specs=pl.BlockSpec((2, n1, tn), lambda b, j: (0, b, j)),
        out_shape=jax.ShapeDtypeStruct((2, batch * n1, lanes), BF16),
        compiler_params=_cparams("parallel", "parallel"),
        name="dft_stage1",
    )(w1, v2)
    y4 = y.reshape(2, batch * n1, n2, cols)
    groups = n1 // kb
    o = pl.pallas_call(
        functools.partial(_dft2_kernel, kb=kb, cols=cols),
        grid=(batch, groups),
        in_specs=[pl.BlockSpec((kb, n2, 2 * n2), lambda b, kg: (kg, 0, 0)),
                  pl.BlockSpec((2, kb, n2, cols), lambda b, kg: (0, b * groups + kg, 0, 0))],
        out_specs=pl.BlockSpec((n2, kb * cols), lambda b, kg: (b, kg)),
        out_shape=jax.ShapeDtypeStruct((n_all // n1, n1 * cols), BF16),
        compiler_params=_cparams("parallel", "parallel"),
        name="dft_stage2",
    )(m2, y4)
    o = o.reshape(n_all, cols)
    if not with_ctx:
        return o
    blk0 = batch * seq // ctx_len
    return pl.pallas_call(
        _dense_dft_kernel,
        grid=(batch,),
        in_specs=[pl.BlockSpec((ctx_len, 2 * ctx_len), lambda b: (0, 0)),
                  pl.BlockSpec((2, ctx_len, cols), lambda b: (0, blk0 + b, 0)),
                  pl.BlockSpec(memory_space=pl.ANY)],
        out_specs=pl.BlockSpec((ctx_len, cols), lambda b: (blk0 + b, 0)),
        out_shape=jax.ShapeDtypeStruct((n_all, cols), BF16),
        input_output_aliases={2: 0},
        compiler_params=_cparams("parallel"),
        name="dft_ctx",
    )(mc, v, o)


def _out_proj_kernel(oa_ref, ob_ref, of_ref, w_ref, x_ref, g_ref, lg_ref, lb_ref,
                     sh_ref, sc_ref, xo_ref, ho_ref, *, alpha, row_chunk):
    for r in range(oa_ref.shape[0] // row_chunk):
        rows = slice(r * row_chunk, (r + 1) * row_chunk)
        mixed = jnp.concatenate([oa_ref[rows, :], ob_ref[rows, :], of_ref[rows, :]], axis=1)
        y = _dot(mixed, w_ref[...])
        xn = _post_norm(x_ref[rows, :], y, g_ref[0], lg_ref[...], lb_ref[...], alpha)
        xo_ref[rows, :] = xn
        ho_ref[rows, :] = _modulate(xn, sh_ref[0], sc_ref[0]).astype(ho_ref.dtype)


def _out_proj(oa, ob, of, w_out, x_all, mod3, lg, lb, *, row_fn, n_rows, tm, alpha):
    n_all, d = x_all.shape
    wa, wb, wf = oa.shape[1], ob.shape[1], of.shape[1]
    assert wa + wb + wf == w_out.shape[0]
    row = lambda i: (i, 0)
    vec = pl.BlockSpec((1, d), lambda i: (0, 0))
    return pl.pallas_call(
        functools.partial(_out_proj_kernel, alpha=alpha, row_chunk=min(tm, 256)),
        grid=(n_rows // tm,),
        in_specs=[pl.BlockSpec((tm, wa), row), pl.BlockSpec((tm, wb), row), pl.BlockSpec((tm, wf), row),
                  pl.BlockSpec(w_out.shape, lambda i: (0, 0)),
                  pl.BlockSpec((tm, d), row),
                  _mod_spec(d, 2, row_fn), vec, vec, _mod_spec(d, 3, row_fn), _mod_spec(d, 4, row_fn)],
        out_specs=[pl.BlockSpec((tm, d), row), pl.BlockSpec((tm, d), row)],
        out_shape=[jax.ShapeDtypeStruct((n_all, d), F32), jax.ShapeDtypeStruct((n_all, d), BF16)],
        compiler_params=_cparams("parallel"),
        name="out_proj",
    )(oa, ob, of, w_out, x_all, mod3, lg, lb, mod3, mod3)


HALO = 16


def _ffn_kernel(hp_ref, hm_ref, hn_ref, wu_ref, wg_ref, cw_ref, cb_ref, wd_ref, x_ref, g_ref, lg_ref,
                lb_ref, *rest, tm, n_lat_tiles, seq, ctx_len, alpha, emit_h, n_split=2):
    if emit_h:
        sh_ref, sc_ref, xo_ref, ho_ref, hext_ref, acc_ref = rest
    else:
        xo_ref, hext_ref, acc_ref = rest
    i = pl.program_id(0)
    j = pl.program_id(1)

    @pl.when(j == 0)
    def _():
        hext_ref[0:HALO] = hp_ref[...]
        hext_ref[HALO:HALO + tm] = hm_ref[...]
        hext_ref[HALO + tm:] = hn_ref[...]
        acc_ref[...] = jnp.zeros_like(acc_ref)

    tf = wu_ref.shape[1]
    th = tf // n_split
    period = jnp.where(i < n_lat_tiles, seq, ctx_len)
    pos = (i * tm + lax.broadcasted_iota(jnp.int32, (tm, th), 0)) & (period - 1)
    first_row, last_row = pos == 0, pos == period - 1
    h_ext, h_main = hext_ref[...], hm_ref[...]
    part = None
    for s in range(n_split):
        sl = slice(s * th, (s + 1) * th)
        gext = _dot(h_ext, wg_ref[:, sl])
        u = _dot(h_main, wu_ref[:, sl])
        g_prev = jnp.where(first_row, 0.0, gext[HALO - 1:HALO - 1 + tm])
        g_cur = gext[HALO:HALO + tm]
        g_next = jnp.where(last_row, 0.0, gext[HALO + 1:HALO + 1 + tm])
        cw = cw_ref[:, sl]
        gc = cb_ref[:, sl] + cw[0:1] * g_prev + cw[1:2] * g_cur + cw[2:3] * g_next
        act = (gc * jax.nn.sigmoid(gc) * u).astype(BF16)
        d_part = _dot(act, wd_ref[sl, :])
        part = d_part if part is None else part + d_part
    acc_ref[...] += part

    @pl.when(j == pl.num_programs(1) - 1)
    def _():
        xn = _post_norm(x_ref[...], acc_ref[...], g_ref[0], lg_ref[...], lb_ref[...], alpha)
        xo_ref[...] = xn
        if emit_h:
            ho_ref[...] = _modulate(xn, sh_ref[0], sc_ref[0]).astype(ho_ref.dtype)


def _conv_ffn(h, w_up, w_gate, conv_w, conv_b, w_down, x_all, mod3, lg, lb, mod3_next, *, row_fn,
              n_rows, n_out_rows, tm, tf, seq, ctx_len, n_lat_tiles, alpha):
    n_all, d = x_all.shape
    d_ff = w_up.shape[1]
    emit_h = mod3_next is not None
    per = tm // HALO
    last_halo = n_rows // HALO - 1
    row = lambda i, j: (i, 0)
    vec = pl.BlockSpec((1, d), lambda i, j: (0, 0))
    in_specs = [pl.BlockSpec((HALO, d), lambda i, j: (jnp.maximum(i * per - 1, 0), 0)),
                pl.BlockSpec((tm, d), row),
                pl.BlockSpec((HALO, d), lambda i, j: (jnp.minimum((i + 1) * per, last_halo), 0)),
                pl.BlockSpec((d, tf), lambda i, j: (0, j)),
                pl.BlockSpec((d, tf), lambda i, j: (0, j)),
                pl.BlockSpec((conv_w.shape[0], tf), lambda i, j: (0, j)),
                pl.BlockSpec((1, tf), lambda i, j: (0, j)),
                pl.BlockSpec((tf, d), lambda i, j: (j, 0)),
                pl.BlockSpec((tm, d), row),
                _mod_spec(d, 5, row_fn), vec, vec]
    args = [h, h, h, w_up, w_gate, conv_w, conv_b, w_down, x_all, mod3, lg, lb]
    out_specs = [pl.BlockSpec((tm, d), row)]
    out_shape = [jax.ShapeDtypeStruct((n_out_rows, d), F32)]
    if emit_h:
        in_specs += [_mod_spec(d, 0, row_fn), _mod_spec(d, 1, row_fn)]
        args += [mod3_next, mod3_next]
        out_specs.append(pl.BlockSpec((tm, d), row))
        out_shape.append(jax.ShapeDtypeStruct((n_out_rows, d), BF16))
    kern = functools.partial(_ffn_kernel, tm=tm, n_lat_tiles=n_lat_tiles, seq=seq, ctx_len=ctx_len,
                             alpha=alpha, emit_h=emit_h)
    return pl.pallas_call(
        kern,
        grid=(n_rows // tm, d_ff // tf),
        in_specs=in_specs,
        out_specs=out_specs,
        out_shape=out_shape,
        scratch_shapes=[pltpu.VMEM((tm + 2 * HALO, d), BF16), pltpu.VMEM((tm, d), F32)],
        compiler_params=_cparams("parallel", "arbitrary"),
        name="conv_ffn",
    )(*args)


def _rope_tables(seq, pad_rows):
    rows = seq // GRID_W
    row = jnp.repeat(jnp.arange(rows), GRID_W).astype(F32)
    col = jnp.tile(jnp.arange(GRID_W), rows).astype(F32)
    n_freq = HEAD_DIM // 4
    inv_freq = ROPE_BASE ** (-jnp.arange(n_freq, dtype=F32) / n_freq)
    ar, ac = row[:, None] * inv_freq, col[:, None] * inv_freq
    zero = jnp.zeros_like(ar)
    cos = jnp.concatenate([jnp.cos(ar), jnp.cos(ar), jnp.cos(ac), jnp.cos(ac)], axis=1)
    s1 = jnp.concatenate([-jnp.sin(ar), zero, -jnp.sin(ac), zero], axis=1)
    s2 = jnp.concatenate([zero, jnp.sin(ar), zero, jnp.sin(ac)], axis=1)
    pad = lambda t, v: jnp.concatenate([t, jnp.full((pad_rows, HEAD_DIM), v, F32)], axis=0)
    return pad(cos, 1.0), pad(s1, 0.0), pad(s2, 0.0)


def _forward(x, c, ctx, c_ctx, w_mod, b_mod, w_in, q_gain_a, k_gain_a, sink_b, w_fourier, w_out,
             ln1_g, ln1_b, w_up, w_gate, conv_w, conv_b, w_down, ln2_g, ln2_b,
             *, tm=512, tm_in=512, tn_in=512, tf=512, tq=512, tk=512):
    batch, seq, d = x.shape
    ctx_len = ctx.shape[1]
    depth = w_mod.shape[0]
    n_lat, n_ctx = batch * seq, batch * ctx_len
    n_all = n_lat + n_ctx
    qa_w, qb_w, f_w = d // 2, d // 4, d // 4
    ka_w, kb_w = A_KV_HEADS * HEAD_DIM, B_KV_HEADS * HEAD_DIM
    assert f_w == F_GROUPS * HEAD_DIM and batch + 1 <= MOD_ROWS
    assert seq & (seq - 1) == 0 and ctx_len & (ctx_len - 1) == 0
    widths = (("qa", qa_w), ("qb", qb_w), ("ka", ka_w), ("v", ka_w), ("kb", kb_w), ("v", kb_w), ("f", f_w))
    kinds = tuple(kind for kind, w in widths for _ in range(w // HEAD_DIM))
    qb_off = qa_w
    ka_off = qa_w + qb_w
    kb_off = ka_off + 2 * ka_w
    f_off = kb_off + 2 * kb_w
    alpha = (2.0 * depth) ** 0.25
    scale = HEAD_DIM ** -0.5 * LOG2_E

    def row_fn_for(t):
        return lambda i: jnp.where(i < n_lat // t, i // (seq // t), batch)

    def rope_fn(i):
        return jnp.where(i < n_lat // tm_in, i % (seq // tm_in), seq // tm_in)

    cc = jnp.concatenate([c, c_ctx[None], jnp.zeros((MOD_ROWS - batch - 1, d), F32)], axis=0)
    mod = _mod_all(cc, w_mod, b_mod)
    wfold = _fourier_fold(w_fourier)
    cos, s1, s2 = _rope_tables(seq, tm_in)
    tables = _dft_tables(seq, ctx_len)
    mod3 = [mod[l].reshape(MOD_ROWS, 1, 6 * d) for l in range(depth)]
    vec = lambda t: t.reshape(1, -1)

    x_cur, h = _ln_mod(x.reshape(n_lat, d), ctx.reshape(n_ctx, d), mod3[0], row_fn_for(tm), tm)
    for l in range(depth):
        last = l == depth - 1
        w_in_b, w_out_b = w_in[l].astype(BF16), w_out[l].astype(BF16)
        w_up_b, w_gate_b, w_down_b = w_up[l].astype(BF16), w_gate[l].astype(BF16), w_down[l].astype(BF16)
        p, vt3 = _in_proj(h, w_in_b, vec(q_gain_a[l]), vec(k_gain_a[l]), cos, s1, s2, kinds=kinds,
                          rope_fn=rope_fn, tm=tm_in, tn=tn_in, scale=scale)
        common = dict(batch=batch, seq=seq, ctx_len=ctx_len, tq=tq, tk=tk)
        ga = dict(window=False, q_off=0, k_off=ka_off, v_row=0, group=qa_w // ka_w, out_width=qa_w, **common)
        gb = dict(window=True, q_off=qb_off, k_off=kb_off, v_row=A_KV_HEADS, group=qb_w // kb_w,
                  out_width=qb_w, **common)
        o_a = _attention(p, vt3, None, None, latent_queries=True, **ga)
        o_b = _attention(p, vt3, None, sink_b[l], latent_queries=True, **gb)
        v = _fourier_channels(p, wfold[l], f_off, 2 * tm if n_all % (2 * tm) == 0 else tm)
        if not last:
            o_a = _attention(p, vt3, o_a, None, latent_queries=False, **ga)
            o_b = _attention(p, vt3, o_b, sink_b[l], latent_queries=False, **gb)
        o_f = _fourier_positions(v, tables, batch=batch, seq=seq, ctx_len=ctx_len, with_ctx=not last)
        n_rows = n_lat if last else n_all
        x_mid, h_mid = _out_proj(o_a, o_b, o_f, w_out_b, x_cur, mod3[l], vec(ln1_g[l]), vec(ln1_b[l]),
                                 row_fn=row_fn_for(tm), n_rows=n_rows, tm=tm, alpha=alpha)
        outs = _conv_ffn(h_mid, w_up_b, w_gate_b, conv_w[l], vec(conv_b[l]), w_down_b, x_mid,
                         mod3[l], vec(ln2_g[l]), vec(ln2_b[l]), None if last else mod3[l + 1],
                         row_fn=row_fn_for(tm), n_rows=n_rows, n_out_rows=n_rows, tm=tm, tf=tf, seq=seq,
                         ctx_len=ctx_len, n_lat_tiles=n_lat // tm, alpha=alpha)
        if last:
            x_cur = outs[0]
        else:
            x_cur, h = outs
    return x_cur.reshape(batch, seq, d)


def kernel(x, c, ctx, c_ctx, w_mod, b_mod, w_in, q_gain_a, k_gain_a, sink_b, w_fourier, w_out, ln1_g, ln1_b,
           w_up, w_gate, conv_w, conv_b, w_down, ln2_g, ln2_b):
    return _forward(x, c, ctx, c_ctx, w_mod, b_mod, w_in, q_gain_a, k_gain_a, sink_b, w_fourier, w_out,
                    ln1_g, ln1_b, w_up, w_gate, conv_w, conv_b, w_down, ln2_g, ln2_b)
```

```python
import functools
import math

import numpy as np
import jax
import jax.numpy as jnp
from jax import lax
from jax.experimental import pallas as pl
from jax.experimental.pallas import tpu as pltpu

F32 = jnp.float32
BF16 = jnp.bfloat16

HEAD_DIM = 128
A_KV_HEADS = 2
B_KV_HEADS = 2
F_GROUPS = 4
WINDOW = 128
GRID_W = 64
ROPE_BASE = 10000.0
EPS = 1e-6
NEG_INF = -1e30
LOG2_E = math.log2(math.e)
MOD_ROWS = 16
VMEM_LIMIT_BYTES = 56 * 1024 * 1024


def _cparams(*sem):
    return pltpu.CompilerParams(dimension_semantics=sem, vmem_limit_bytes=VMEM_LIMIT_BYTES)


def _dot(a, b):
    return jnp.dot(a, b, preferred_element_type=F32)


def _ln(v):
    mu = jnp.mean(v, axis=-1, keepdims=True)
    d = v - mu
    var = jnp.mean(d * d, axis=-1, keepdims=True)
    return d * lax.rsqrt(var + EPS)


def _post_norm(x, y, gate, lg, lb, alpha):
    return _ln(alpha * x + gate * y) * lg + lb


def _modulate(x, sh, sc):
    return _ln(x) * (1.0 + sc) + sh


def _mod_kernel(c_ref, w_ref, b_ref, o_ref):
    a = c_ref[...]
    a = a * jax.nn.sigmoid(a)
    o_ref[0] = jnp.dot(a, w_ref[0], preferred_element_type=F32,
                       precision=lax.Precision.HIGHEST) + b_ref[0]


def _mod_all(cc, w_mod, b_mod, tn=1024):
    depth, d, n = w_mod.shape
    return pl.pallas_call(
        _mod_kernel,
        grid=(depth, n // tn),
        in_specs=[pl.BlockSpec((MOD_ROWS, d), lambda l, j: (0, 0)),
                  pl.BlockSpec((1, d, tn), lambda l, j: (l, 0, j)),
                  pl.BlockSpec((1, 1, tn), lambda l, j: (l, 0, j))],
        out_specs=pl.BlockSpec((1, MOD_ROWS, tn), lambda l, j: (l, 0, j)),
        out_shape=jax.ShapeDtypeStruct((depth, MOD_ROWS, n), F32),
        compiler_params=_cparams("parallel", "parallel"),
        name="mod_all",
    )(cc, w_mod, b_mod.reshape(depth, 1, n))


def _fold_kernel(cs_ref, w_ref, o_ref):
    o_ref[0] = jnp.dot(cs_ref[...], w_ref[0], preferred_element_type=F32,
                       precision=lax.Precision.HIGHEST)


def _fourier_fold(w_fourier):
    depth, groups, c, _ = w_fourier.shape
    k = np.arange(c)
    ang = 2.0 * np.pi * np.outer(k, k) / c
    cs = np.concatenate([np.cos(ang), -np.sin(ang)], axis=0) / math.sqrt(c)
    out = pl.pallas_call(
        _fold_kernel,
        grid=(depth * groups,),
        in_specs=[pl.BlockSpec((2 * c, c), lambda i: (0, 0)),
                  pl.BlockSpec((1, c, c), lambda i: (i, 0, 0))],
        out_specs=pl.BlockSpec((1, 2 * c, c), lambda i: (i, 0, 0)),
        out_shape=jax.ShapeDtypeStruct((depth * groups, 2 * c, c), F32),
        compiler_params=_cparams("parallel"),
        name="fourier_fold",
    )(jnp.asarray(cs, F32), w_fourier.reshape(depth * groups, c, c))
    out = out.reshape(depth, groups, 2, c, c).transpose(0, 1, 3, 2, 4).reshape(depth, groups, c, 2 * c)
    return out.astype(BF16)


def _ln_mod_kernel(x_ref, c_ref, sh_ref, sc_ref, xa_ref, h_ref, *, n_lat_tiles):
    latent = pl.program_id(0) < n_lat_tiles

    @pl.when(latent)
    def _():
        xa_ref[...] = x_ref[...]

    @pl.when(jnp.logical_not(latent))
    def _():
        xa_ref[...] = c_ref[...]

    h_ref[...] = _modulate(xa_ref[...], sh_ref[0], sc_ref[0]).astype(h_ref.dtype)


def _mod_spec(d, chunk, row_fn):
    return pl.BlockSpec((1, 1, d), lambda i, *_: (row_fn(i), 0, chunk))


def _ln_mod(x_lat, x_ctx, mod3, row_fn, tm):
    n_lat, d = x_lat.shape
    n_ctx = x_ctx.shape[0]
    n_lat_tiles, n = n_lat // tm, n_lat + n_ctx
    return pl.pallas_call(
        functools.partial(_ln_mod_kernel, n_lat_tiles=n_lat_tiles),
        grid=(n // tm,),
        in_specs=[pl.BlockSpec((tm, d), lambda i: (jnp.minimum(i, n_lat_tiles - 1), 0)),
                  pl.BlockSpec((tm, d), lambda i: (jnp.maximum(i - n_lat_tiles, 0), 0)),
                  _mod_spec(d, 0, row_fn), _mod_spec(d, 1, row_fn)],
        out_specs=[pl.BlockSpec((tm, d), lambda i: (i, 0)), pl.BlockSpec((tm, d), lambda i: (i, 0))],
        out_shape=[jax.ShapeDtypeStruct((n, d), F32), jax.ShapeDtypeStruct((n, d), BF16)],
        compiler_params=_cparams("arbitrary"),
        name="ln_mod",
    )(x_lat, x_ctx, mod3, mod3)


def _rope(t, cos, s1, s2):
    return t * cos + pltpu.roll(t, HEAD_DIM - 32, 1) * s1 + pltpu.roll(t, 32, 1) * s2


def _rms(t, gain):
    return t * lax.rsqrt(jnp.mean(t * t, axis=-1, keepdims=True) + EPS) * gain


def _win_kernel(h_ref, w_ref, qg_ref, kg_ref, cos_ref, s1_ref, s2_ref, p_ref, vt_ref, *,
                kinds, tm, tn, scale):
    h = h_ref[...]
    cos, s1, s2 = cos_ref[...], s1_ref[...], s2_ref[...]
    per = tn // HEAD_DIM
    n_v = 0
    for b in range(len(kinds) // per):
        acc = _dot(h, w_ref[:, b * tn:(b + 1) * tn])
        for c in range(per):
            kind = kinds[b * per + c]
            col = (b * per + c) * HEAD_DIM
            t = acc[:, c * HEAD_DIM:(c + 1) * HEAD_DIM]
            if kind == "qa":
                t = _rope(_rms(t, qg_ref[...]), cos, s1, s2) * scale
            elif kind == "qb":
                t = _rope(t, cos, s1, s2) * scale
            elif kind == "ka":
                t = _rope(_rms(t, kg_ref[...]), cos, s1, s2)
            elif kind == "kb":
                t = _rope(t, cos, s1, s2)
            p_ref[:, col:col + HEAD_DIM] = t.astype(p_ref.dtype)
            if kind == "v":
                for r in range(tm // HEAD_DIM):
                    tile = t[r * HEAD_DIM:(r + 1) * HEAD_DIM]
                    vt_ref[r, n_v * HEAD_DIM:(n_v + 1) * HEAD_DIM, :] = tile.T.astype(vt_ref.dtype)
                n_v += 1


def _in_proj(h, w, qg, kg, cos, s1, s2, *, layer, kinds, rope_fn, tm, tn, scale):
    n, d = h.shape
    in_w = w.shape[2]
    n_v = kinds.count("v")
    kern = functools.partial(_win_kernel, kinds=kinds, tm=tm, tn=tn, scale=scale)
    tab = pl.BlockSpec((tm, HEAD_DIM), lambda i: (rope_fn(i), 0))
    vec = pl.BlockSpec((1, HEAD_DIM), lambda i: (0, 0))
    return pl.pallas_call(
        kern,
        grid=(n // tm,),
        in_specs=[pl.BlockSpec((tm, d), lambda i: (i, 0)),
                  pl.BlockSpec((None, d, in_w), lambda i: (layer, 0, 0)),
                  vec, vec, tab, tab, tab],
        out_specs=[pl.BlockSpec((tm, in_w), lambda i: (i, 0)),
                   pl.BlockSpec((tm // HEAD_DIM, n_v * HEAD_DIM, HEAD_DIM), lambda i: (i, 0, 0))],
        out_shape=[jax.ShapeDtypeStruct((n, in_w), BF16),
                   jax.ShapeDtypeStruct((n // HEAD_DIM, n_v * HEAD_DIM, HEAD_DIM), BF16)],
        compiler_params=_cparams("parallel"),
        name="in_proj",
    )(h, w, qg, kg, cos, s1, s2)


def _scores_t(k, q):
    return lax.dot_general(k, q, (((1,), (1,)), ((), ())), preferred_element_type=F32)


def _vt_tile(vt_ref, first, count):
    tiles = [vt_ref[first + t] for t in range(count)]
    return tiles[0] if count == 1 else jnp.concatenate(tiles, axis=1)


def _online_first(s, vt):
    m = jnp.max(s, axis=0, keepdims=True)
    p = jnp.exp2(s - m)
    return m, jnp.sum(p, axis=0, keepdims=True), _dot(vt, p.astype(BF16))


def _exact_zero(acc):
    bits = pltpu.bitcast(acc[0:1, 0:HEAD_DIM], jnp.uint32)
    bits = lax.shift_right_logical(lax.shift_right_logical(bits, jnp.uint32(16)), jnp.uint32(16))
    return pltpu.bitcast(bits, F32).astype(BF16)


SUM_ROWS = 16


def _with_ones(vt):
    return jnp.concatenate([vt, jnp.ones((SUM_ROWS, vt.shape[1]), vt.dtype)], axis=0)


def _online_step(s, vt_ones, m, acc):
    m_new = jnp.maximum(m, jnp.max(s, axis=0, keepdims=True))
    p = jnp.exp2(s - m_new).astype(BF16)
    return m_new, jnp.exp2(m - m_new) * acc + _dot(vt_ones, p)


def _global_attn_kernel(*refs, group, tk, n_chunks, ctx_len):
    if n_chunks:
        q_ref, k_ref, vt_ref, kc_ref, vtc_ref, o_ref = refs[:6]
        s_refs = refs[6:6 + 2 * group]
        acc_refs = refs[6 + 2 * group:]
    else:
        q_ref, kc_ref, vtc_ref, o_ref = refs
    sub = tk // HEAD_DIM
    tq = q_ref.shape[0]
    cols = [slice(g * HEAD_DIM, (g + 1) * HEAD_DIM) for g in range(group)]
    qs = [q_ref[:, c] for c in cols]
    kc, vtc = kc_ref[...], _vt_tile(vtc_ref, 0, ctx_len // HEAD_DIM)
    if not n_chunks:
        for c, q in zip(cols, qs):
            _, l, acc = _online_first(_scores_t(kc, q), vtc)
            o_ref[:, c] = (acc / l).T.astype(o_ref.dtype)
    else:
        assert n_chunks % 2 == 0

        def lat_keys(c):
            return k_ref[pl.ds(pl.multiple_of(c * tk, tk), tk), :]

        def lat_values(c):
            return _with_ones(_vt_tile(vt_ref, c * sub, sub))

        def stage(k_next, slot_next, vt_cur, slot_cur, ms):
            out, gate = [], None
            for g, q in enumerate(qs):
                if k_next is not None:
                    n_keys = k_next.shape[0]
                    s_refs[slot_next * group + g][0:n_keys, :] = _scores_t(
                        k_next if gate is None else k_next + gate, q)
                if vt_cur is not None:
                    n_keys = vt_cur.shape[1]
                    m_new, acc = _online_step(s_refs[slot_cur * group + g][0:n_keys, :], vt_cur, ms[g],
                                              acc_refs[g][...])
                    acc_refs[g][...] = acc
                    gate = _exact_zero(m_new)
                    out.append(m_new)
            return tuple(out)

        def body(i, ms):
            ms = stage(lat_keys(2 * i + 1), 1, lat_values(2 * i), 0, ms)
            return stage(lat_keys(2 * i + 2), 0, lat_values(2 * i + 1), 1, ms)

        for acc_ref in acc_refs:
            acc_ref[...] = jnp.zeros_like(acc_ref)
        ms = (jnp.full((1, tq), -jnp.inf, F32),) * group
        stage(lat_keys(0), 0, None, None, None)
        ms = lax.fori_loop(0, n_chunks // 2 - 1, body, ms)
        ms = stage(lat_keys(n_chunks - 1), 1, lat_values(n_chunks - 2), 0, ms)
        ms = stage(kc, 0, lat_values(n_chunks - 1), 1, ms)
        stage(None, None, _with_ones(vtc), 0, ms)
        for c, acc_ref in zip(cols, acc_refs):
            o = acc_ref[0:HEAD_DIM, :] / acc_ref[HEAD_DIM:HEAD_DIM + 1, :]
            o_ref[:, c] = o.T.astype(o_ref.dtype)


def _window_attn_kernel(*refs, group, tq, seq, ctx_len, local):
    if local:
        sink_ref, bias_ref, q_ref, k_ref, vt_ref, kc_ref, vtc_ref, o_ref = refs
    else:
        sink_ref, q_ref, kc_ref, vtc_ref, o_ref = refs
    kv = pl.program_id(1)
    qi = pl.program_id(2)
    sub = tq // HEAD_DIM
    n_sub = seq // HEAD_DIM
    kc, vtc = kc_ref[...], _vt_tile(vtc_ref, 0, ctx_len // HEAD_DIM)
    if local:
        first = qi * sub
        prev = jnp.maximum(first - 1, 0)
        nxt = jnp.minimum(first + sub, n_sub - 1)
        k_loc = jnp.concatenate(
            [k_ref[pl.ds(pl.multiple_of(prev * HEAD_DIM, HEAD_DIM), HEAD_DIM), :],
             k_ref[pl.ds(pl.multiple_of(first * HEAD_DIM, tq), tq), :],
             k_ref[pl.ds(pl.multiple_of(nxt * HEAD_DIM, HEAD_DIM), HEAD_DIM), :]], axis=0)
        vt_loc = jnp.concatenate(
            [vt_ref[prev]] + [vt_ref[first + t] for t in range(sub)] + [vt_ref[nxt]], axis=1)
        bias = bias_ref[...]

    for g in range(group):
        cols = slice(g * HEAD_DIM, (g + 1) * HEAD_DIM)
        q = q_ref[:, cols]
        sink = sink_ref[kv * group + g] * LOG2_E
        s_c = _scores_t(kc, q)
        m = jnp.maximum(jnp.max(s_c, axis=0, keepdims=True), sink)
        if local:
            s_l = _scores_t(k_loc, q) + bias
            m = jnp.maximum(m, jnp.max(s_l, axis=0, keepdims=True))
        p_c = jnp.exp2(s_c - m)
        denom = jnp.sum(p_c, axis=0, keepdims=True) + jnp.exp2(sink - m)
        acc = _dot(vtc, p_c.astype(BF16))
        if local:
            p_l = jnp.exp2(s_l - m)
            denom = denom + jnp.sum(p_l, axis=0, keepdims=True)
            acc = acc + _dot(vt_loc, p_l.astype(BF16))
        o_ref[:, cols] = (acc / denom).T.astype(o_ref.dtype)


def _window_bias(n_loc, tq):
    key = jnp.arange(n_loc)[:, None] - HEAD_DIM
    band = jnp.abs(key - jnp.arange(tq)[None, :]) <= WINDOW
    variants = [band & (key >= 0), band, band & (key < tq)]
    return jnp.stack([jnp.where(v, 0.0, NEG_INF).astype(F32) for v in variants])


def _attention(p, vt3, out_prev, sink, *, window, latent_queries, batch, seq, ctx_len, q_off, k_off,
               v_row, group, out_width, tq, tk):
    n_all = p.shape[0]
    gw = group * HEAD_DIM
    ctx_blk0 = batch * seq // ctx_len
    kc_spec = pl.BlockSpec((ctx_len, HEAD_DIM), lambda b, kv, qi, *_: (ctx_blk0 + b, k_off // HEAD_DIM + kv))
    vtc_spec = pl.BlockSpec((ctx_len // HEAD_DIM, HEAD_DIM, HEAD_DIM),
                            lambda b, kv, qi, *_: (ctx_blk0 + b, v_row + kv, 0))
    if latent_queries:
        n_q = seq // tq
        q_blk = lambda b, kv, qi, *_: (b * n_q + qi, q_off // gw + kv)
        lat_specs = [pl.BlockSpec((seq, HEAD_DIM), lambda b, kv, qi, *_: (b, k_off // HEAD_DIM + kv)),
                     pl.BlockSpec((seq // HEAD_DIM, HEAD_DIM, HEAD_DIM), lambda b, kv, qi, *_: (b, v_row + kv, 0))]
        lat_args = [p, vt3]
    else:
        tq = ctx_len
        n_q = 1
        q_blk = lambda b, kv, qi, *_: (ctx_blk0 + b, q_off // gw + kv)
        lat_specs, lat_args = [], []
    q_spec = pl.BlockSpec((tq, gw), q_blk)
    o_spec = pl.BlockSpec((tq, gw), lambda b, kv, qi, *_: (q_blk(b, kv, qi)[0], kv))
    if window:
        kern = functools.partial(_window_attn_kernel, group=group, tq=tq, seq=seq, ctx_len=ctx_len,
                                 local=latent_queries)
        pre_specs = [pl.BlockSpec(memory_space=pltpu.SMEM)]
        pre_args = [sink]
        if latent_queries:
            assert n_q >= 2
            n_loc = tq + 2 * HEAD_DIM
            pre_specs.append(pl.BlockSpec(
                (None, n_loc, tq), lambda b, kv, qi: (jnp.where(qi == 0, 0, jnp.where(qi == n_q - 1, 2, 1)), 0, 0)))
            pre_args.append(_window_bias(n_loc, tq))
    else:
        kern = functools.partial(_global_attn_kernel, group=group, tk=tk,
                                 n_chunks=seq // tk if latent_queries else 0, ctx_len=ctx_len)
        pre_specs, pre_args = [], []
    scratch = []
    if latent_queries and not window:
        assert ctx_len <= tk
        scratch = [pltpu.VMEM((tk, tq), F32)] * (2 * group) + [pltpu.VMEM((HEAD_DIM + SUM_ROWS, tq), F32)] * group
    in_specs = pre_specs + [q_spec] + lat_specs + [kc_spec, vtc_spec]
    args = pre_args + [p] + lat_args + [p, vt3]
    aliases = {}
    if out_prev is not None:
        in_specs.append(pl.BlockSpec(memory_space=pl.ANY))
        args.append(out_prev)
        aliases = {len(args) - 1: 0}
        body = kern

        def kern(*refs):
            return body(*refs[:-2], refs[-1])
    return pl.pallas_call(
        kern,
        grid=(batch, A_KV_HEADS, n_q),
        in_specs=in_specs,
        out_specs=o_spec,
        out_shape=jax.ShapeDtypeStruct((n_all, out_width), BF16),
        input_output_aliases=aliases,
        scratch_shapes=scratch,
        compiler_params=_cparams("parallel", "parallel", "arbitrary"),
        name=("window" if window else "global") + ("_lat" if latent_queries else "_ctx"),
    )(*args)


def _chan_kernel(u_ref, w_ref, v_ref):
    groups, c, _ = w_ref.shape
    for g in range(groups):
        cols = slice(g * c, (g + 1) * c)
        r = _dot(u_ref[:, cols], w_ref[g])
        v_ref[0, :, cols] = r[:, :c].astype(v_ref.dtype)
        v_ref[1, :, cols] = r[:, c:].astype(v_ref.dtype)


def _fourier_channels(p, wfold_l, f_off, tm):
    n = p.shape[0]
    groups, c, _ = wfold_l.shape
    f_w = groups * c
    assert f_off % f_w == 0
    return pl.pallas_call(
        _chan_kernel,
        grid=(n // tm,),
        in_specs=[pl.BlockSpec((tm, f_w), lambda i: (i, f_off // f_w)),
                  pl.BlockSpec((groups, c, 2 * c), lambda i: (0, 0, 0))],
        out_specs=pl.BlockSpec((2, tm, f_w), lambda i: (0, i, 0)),
        out_shape=jax.ShapeDtypeStruct((2, n, f_w), BF16),
        compiler_params=_cparams("parallel"),
        name="fourier_channels",
    )(p, wfold_l)


def _dft1_kernel(w_ref, v_ref, y_ref, *, cols):
    x = jnp.concatenate([v_ref[0], v_ref[1]], axis=0)
    y = _dot(w_ref[...], x)
    n1 = y.shape[0] // 2
    for part in range(2):
        for e in range(y.shape[1] // cols):
            y_ref[part, :, e, :] = y[part * n1:(part + 1) * n1, e * cols:(e + 1) * cols]


def _dft2_kernel(m_ref, y_ref, o_ref, *, kb):
    for kk in range(kb):
        x = jnp.concatenate([y_ref[0, kk], y_ref[1, kk]], axis=0).astype(BF16)
        o_ref[:, kk, :] = _dot(m_ref[kk], x)


def _dense_dft_kernel(m_ref, v_ref, prev_ref, o_ref):
    del prev_ref
    x = jnp.concatenate([v_ref[0], v_ref[1]], axis=0)
    o_ref[...] = _dot(m_ref[...], x).astype(o_ref.dtype)


def _dft_tables(seq, ctx_len):
    n1 = n2 = int(round(math.sqrt(seq)))
    assert n1 * n2 == seq
    a1 = 2.0 * np.pi * np.outer(np.arange(n1), np.arange(n1)) / n1
    c1, s1 = np.cos(a1), np.sin(a1)
    w1 = np.block([[c1, s1], [-s1, c1]])
    k = np.arange(n1)[:, None, None] + n1 * np.arange(n2)[None, :, None]
    a2 = 2.0 * np.pi * ((k * np.arange(n2)[None, None, :]) % seq) / seq
    m2 = np.concatenate([np.cos(a2), np.sin(a2)], axis=2) / math.sqrt(seq)
    ac = 2.0 * np.pi * np.outer(np.arange(ctx_len), np.arange(ctx_len)) / ctx_len
    mc = np.concatenate([np.cos(ac), np.sin(ac)], axis=1) / math.sqrt(ctx_len)
    return n1, n2, jnp.asarray(w1, BF16), jnp.asarray(m2, BF16), jnp.asarray(mc, BF16)


def _fourier_positions(v, tables, *, batch, seq, ctx_len, with_ctx, per=8, kb=8):
    n1, n2, w1, m2, mc = tables
    _, n_all, cols = v.shape
    per, kb = min(per, n2), min(kb, n1)
    tn = per * cols
    v_lat = v.reshape(2, n_all // n2, n2 * cols)
    y = pl.pallas_call(
        functools.partial(_dft1_kernel, cols=cols),
        grid=(batch, n2 // per),
        in_specs=[pl.BlockSpec((2 * n1, 2 * n1), lambda b, j: (0, 0)),
                  pl.BlockSpec((2, n1, tn), lambda b, j: (0, b, j))],
        out_specs=pl.BlockSpec((2, n1, per, cols), lambda b, j: (0, b, j, 0)),
        out_shape=jax.ShapeDtypeStruct((2, batch * n1, n2, cols), F32),
        compiler_params=_cparams("parallel", "parallel"),
        name="dft_stage1",
    )(w1, v_lat)
    groups = n1 // kb
    o = pl.pallas_call(
        functools.partial(_dft2_kernel, kb=kb),
        grid=(batch, groups),
        in_specs=[pl.BlockSpec((kb, n2, 2 * n2), lambda b, kg: (kg, 0, 0)),
                  pl.BlockSpec((2, kb, n2, cols), lambda b, kg: (0, b * groups + kg, 0, 0))],
        out_specs=pl.BlockSpec((n2, kb, cols), lambda b, kg: (b, kg, 0)),
        out_shape=jax.ShapeDtypeStruct((n_all // n1, n1, cols), F32),
        compiler_params=_cparams("parallel", "parallel"),
        name="dft_stage2",
    )(m2, y)
    o = o.reshape(n_all, cols)
    if not with_ctx:
        return o
    blk0 = batch * seq // ctx_len
    return pl.pallas_call(
        _dense_dft_kernel,
        grid=(batch,),
        in_specs=[pl.BlockSpec((ctx_len, 2 * ctx_len), lambda b: (0, 0)),
                  pl.BlockSpec((2, ctx_len, cols), lambda b: (0, blk0 + b, 0)),
                  pl.BlockSpec(memory_space=pl.ANY)],
        out_specs=pl.BlockSpec((ctx_len, cols), lambda b: (blk0 + b, 0)),
        out_shape=jax.ShapeDtypeStruct((n_all, cols), F32),
        input_output_aliases={2: 0},
        compiler_params=_cparams("parallel"),
        name="dft_ctx",
    )(mc, v, o)


def _out_proj_kernel(oa_ref, ob_ref, of_ref, w_ref, x_ref, g_ref, lg_ref, lb_ref,
                     sh_ref, sc_ref, xo_ref, ho_ref, *, alpha, row_chunk):
    for r in range(oa_ref.shape[0] // row_chunk):
        rows = slice(r * row_chunk, (r + 1) * row_chunk)
        mixed = jnp.concatenate([oa_ref[rows, :], ob_ref[rows, :], of_ref[rows, :].astype(BF16)], axis=1)
        y = _dot(mixed, w_ref[...])
        xn = _post_norm(x_ref[rows, :], y, g_ref[0], lg_ref[...], lb_ref[...], alpha)
        xo_ref[rows, :] = xn
        ho_ref[rows, :] = _modulate(xn, sh_ref[0], sc_ref[0]).astype(ho_ref.dtype)


def _out_proj(oa, ob, of, w_out, x_all, mod3, lg, lb, *, layer, row_fn, n_rows, tm, alpha):
    n_all, d = x_all.shape
    wa, wb, wf = oa.shape[1], ob.shape[1], of.shape[1]
    assert wa + wb + wf == w_out.shape[1]
    row = lambda i: (i, 0)
    vec = pl.BlockSpec((1, d), lambda i: (0, 0))
    return pl.pallas_call(
        functools.partial(_out_proj_kernel, alpha=alpha, row_chunk=min(tm, 256)),
        grid=(n_rows // tm,),
        in_specs=[pl.BlockSpec((tm, wa), row), pl.BlockSpec((tm, wb), row), pl.BlockSpec((tm, wf), row),
                  pl.BlockSpec((None,) + w_out.shape[1:], lambda i: (layer, 0, 0)),
                  pl.BlockSpec((tm, d), row),
                  _mod_spec(d, 2, row_fn), vec, vec, _mod_spec(d, 3, row_fn), _mod_spec(d, 4, row_fn)],
        out_specs=[pl.BlockSpec((tm, d), row), pl.BlockSpec((tm, d), row)],
        out_shape=[jax.ShapeDtypeStruct((n_all, d), F32), jax.ShapeDtypeStruct((n_all, d), BF16)],
        compiler_params=_cparams("parallel"),
        name="out_proj",
    )(oa, ob, of, w_out, x_all, mod3, lg, lb, mod3, mod3)


HALO = 16


def _ffn_kernel(hp_ref, hm_ref, hn_ref, wu_ref, wg_ref, cw_ref, cb_ref, wd_ref, x_ref, g_ref, lg_ref,
                lb_ref, *rest, tm, n_lat_tiles, seq, ctx_len, alpha, emit_h, n_split=2):
    if emit_h:
        sh_ref, sc_ref, xo_ref, ho_ref, hext_ref, acc_ref = rest
    else:
        xo_ref, hext_ref, acc_ref = rest
    i = pl.program_id(0)
    j = pl.program_id(1)

    @pl.when(j == 0)
    def _():
        hext_ref[0:HALO] = hp_ref[...]
        hext_ref[HALO:HALO + tm] = hm_ref[...]
        hext_ref[HALO + tm:] = hn_ref[...]
        acc_ref[...] = jnp.zeros_like(acc_ref)

    tf = wu_ref.shape[1]
    th = tf // n_split
    period = jnp.where(i < n_lat_tiles, seq, ctx_len)
    pos = (i * tm + lax.broadcasted_iota(jnp.int32, (tm, th), 0)) & (period - 1)
    first_row, last_row = pos == 0, pos == period - 1
    h_ext, h_main = hext_ref[...], hm_ref[...]
    part = None
    for s in range(n_split):
        sl = slice(s * th, (s + 1) * th)
        gext = _dot(h_ext, wg_ref[:, sl])
        u = _dot(h_main, wu_ref[:, sl])
        g_prev = jnp.where(first_row, 0.0, gext[HALO - 1:HALO - 1 + tm])
        g_cur = gext[HALO:HALO + tm]
        g_next = jnp.where(last_row, 0.0, gext[HALO + 1:HALO + 1 + tm])
        cw = cw_ref[:, sl]
        gc = cb_ref[:, sl] + cw[0:1] * g_prev + cw[1:2] * g_cur + cw[2:3] * g_next
        act = (gc * jax.nn.sigmoid(gc) * u).astype(BF16)
        d_part = _dot(act, wd_ref[sl, :])
        part = d_part if part is None else part + d_part
    acc_ref[...] += part

    @pl.when(j == pl.num_programs(1) - 1)
    def _():
        xn = _post_norm(x_ref[...], acc_ref[...], g_ref[0], lg_ref[...], lb_ref[...], alpha)
        xo_ref[...] = xn
        if emit_h:
            ho_ref[...] = _modulate(xn, sh_ref[0], sc_ref[0]).astype(ho_ref.dtype)


def _conv_ffn(h, w_up, w_gate, conv_w, conv_b, w_down, x_all, mod3, lg, lb, mod3_next, *, layer, row_fn,
              n_rows, n_out_rows, tm, tf, seq, ctx_len, n_lat_tiles, alpha):
    n_all, d = x_all.shape
    d_ff = w_up.shape[2]
    emit_h = mod3_next is not None
    per = tm // HALO
    last_halo = n_rows // HALO - 1
    row = lambda i, j: (i, 0)
    vec = pl.BlockSpec((1, d), lambda i, j: (0, 0))
    in_specs = [pl.BlockSpec((HALO, d), lambda i, j: (jnp.maximum(i * per - 1, 0), 0)),
                pl.BlockSpec((tm, d), row),
                pl.BlockSpec((HALO, d), lambda i, j: (jnp.minimum((i + 1) * per, last_halo), 0)),
                pl.BlockSpec((None, d, tf), lambda i, j: (layer, 0, j)),
                pl.BlockSpec((None, d, tf), lambda i, j: (layer, 0, j)),
                pl.BlockSpec((conv_w.shape[0], tf), lambda i, j: (0, j)),
                pl.BlockSpec((1, tf), lambda i, j: (0, j)),
                pl.BlockSpec((None, tf, d), lambda i, j: (layer, j, 0)),
                pl.BlockSpec((tm, d), row),
                _mod_spec(d, 5, row_fn), vec, vec]
    args = [h, h, h, w_up, w_gate, conv_w, conv_b, w_down, x_all, mod3, lg, lb]
    out_specs = [pl.BlockSpec((tm, d), row)]
    out_shape = [jax.ShapeDtypeStruct((n_out_rows, d), F32)]
    if emit_h:
        in_specs += [_mod_spec(d, 0, row_fn), _mod_spec(d, 1, row_fn)]
        args += [mod3_next, mod3_next]
        out_specs.append(pl.BlockSpec((tm, d), row))
        out_shape.append(jax.ShapeDtypeStruct((n_out_rows, d), BF16))
    kern = functools.partial(_ffn_kernel, tm=tm, n_lat_tiles=n_lat_tiles, seq=seq, ctx_len=ctx_len,
                             alpha=alpha, emit_h=emit_h)
    return pl.pallas_call(
        kern,
        grid=(n_rows // tm, d_ff // tf),
        in_specs=in_specs,
        out_specs=out_specs,
        out_shape=out_shape,
        scratch_shapes=[pltpu.VMEM((tm + 2 * HALO, d), BF16), pltpu.VMEM((tm, d), F32)],
        compiler_params=_cparams("parallel", "arbitrary"),
        name="conv_ffn",
    )(*args)


def _rope_tables(seq, pad_rows):
    rows = seq // GRID_W
    row = jnp.repeat(jnp.arange(rows), GRID_W).astype(F32)
    col = jnp.tile(jnp.arange(GRID_W), rows).astype(F32)
    n_freq = HEAD_DIM // 4
    inv_freq = ROPE_BASE ** (-jnp.arange(n_freq, dtype=F32) / n_freq)
    ar, ac = row[:, None] * inv_freq, col[:, None] * inv_freq
    zero = jnp.zeros_like(ar)
    cos = jnp.concatenate([jnp.cos(ar), jnp.cos(ar), jnp.cos(ac), jnp.cos(ac)], axis=1)
    s1 = jnp.concatenate([-jnp.sin(ar), zero, -jnp.sin(ac), zero], axis=1)
    s2 = jnp.concatenate([zero, jnp.sin(ar), zero, jnp.sin(ac)], axis=1)
    pad = lambda t, v: jnp.concatenate([t, jnp.full((pad_rows, HEAD_DIM), v, F32)], axis=0)
    return pad(cos, 1.0), pad(s1, 0.0), pad(s2, 0.0)


def _forward(x, c, ctx, c_ctx, w_mod, b_mod, w_in, q_gain_a, k_gain_a, sink_b, w_fourier, w_out,
             ln1_g, ln1_b, w_up, w_gate, conv_w, conv_b, w_down, ln2_g, ln2_b,
             *, tm=512, tm_in=512, tn_in=512, tf=512, tq=512, tk=512):
    batch, seq, d = x.shape
    ctx_len = ctx.shape[1]
    depth = w_mod.shape[0]
    n_lat, n_ctx = batch * seq, batch * ctx_len
    n_all = n_lat + n_ctx
    qa_w, qb_w, f_w = d // 2, d // 4, d // 4
    ka_w, kb_w = A_KV_HEADS * HEAD_DIM, B_KV_HEADS * HEAD_DIM
    assert f_w == F_GROUPS * HEAD_DIM and batch + 1 <= MOD_ROWS
    assert seq & (seq - 1) == 0 and ctx_len & (ctx_len - 1) == 0
    widths = (("qa", qa_w), ("qb", qb_w), ("ka", ka_w), ("v", ka_w), ("kb", kb_w), ("v", kb_w), ("f", f_w))
    kinds = tuple(kind for kind, w in widths for _ in range(w // HEAD_DIM))
    qb_off = qa_w
    ka_off = qa_w + qb_w
    kb_off = ka_off + 2 * ka_w
    f_off = kb_off + 2 * kb_w
    alpha = (2.0 * depth) ** 0.25
    scale = HEAD_DIM ** -0.5 * LOG2_E

    def row_fn_for(t):
        return lambda i: jnp.where(i < n_lat // t, i // (seq // t), batch)

    def rope_fn(i):
        return jnp.where(i < n_lat // tm_in, i % (seq // tm_in), seq // tm_in)

    cc = jnp.concatenate([c, c_ctx[None], jnp.zeros((MOD_ROWS - batch - 1, d), F32)], axis=0)
    mod = _mod_all(cc, w_mod, b_mod)
    wfold = _fourier_fold(w_fourier)
    cos, s1, s2 = _rope_tables(seq, tm_in)
    tables = _dft_tables(seq, ctx_len)
    mod3 = [mod[l].reshape(MOD_ROWS, 1, 6 * d) for l in range(depth)]
    vec = lambda t: t.reshape(1, -1)

    w_in_b, w_out_b = w_in.astype(BF16), w_out.astype(BF16)
    w_up_b, w_gate_b, w_down_b = w_up.astype(BF16), w_gate.astype(BF16), w_down.astype(BF16)

    x_cur, h = _ln_mod(x.reshape(n_lat, d), ctx.reshape(n_ctx, d), mod3[0], row_fn_for(tm), tm)
    for l in range(depth):
        last = l == depth - 1
        p, vt3 = _in_proj(h, w_in_b, vec(q_gain_a[l]), vec(k_gain_a[l]), cos, s1, s2, layer=l, kinds=kinds,
                          rope_fn=rope_fn, tm=tm_in, tn=tn_in, scale=scale)
        common = dict(batch=batch, seq=seq, ctx_len=ctx_len, tq=tq, tk=tk)
        ga = dict(window=False, q_off=0, k_off=ka_off, v_row=0, group=qa_w // ka_w, out_width=qa_w, **common)
        gb = dict(window=True, q_off=qb_off, k_off=kb_off, v_row=A_KV_HEADS, group=qb_w // kb_w,
                  out_width=qb_w, **common)
        o_a = _attention(p, vt3, None, None, latent_queries=True, **ga)
        o_b = _attention(p, vt3, None, sink_b[l], latent_queries=True, **gb)
        v = _fourier_channels(p, wfold[l], f_off, 2 * tm if n_all % (2 * tm) == 0 else tm)
        if not last:
            o_a = _attention(p, vt3, o_a, None, latent_queries=False, **ga)
            o_b = _attention(p, vt3, o_b, sink_b[l], latent_queries=False, **gb)
        o_f = _fourier_positions(v, tables, batch=batch, seq=seq, ctx_len=ctx_len, with_ctx=not last)
        n_rows = n_lat if last else n_all
        x_mid, h_mid = _out_proj(o_a, o_b, o_f, w_out_b, x_cur, mod3[l], vec(ln1_g[l]), vec(ln1_b[l]),
                                 layer=l, row_fn=row_fn_for(tm), n_rows=n_rows, tm=tm, alpha=alpha)
        outs = _conv_ffn(h_mid, w_up_b, w_gate_b, conv_w[l], vec(conv_b[l]), w_down_b, x_mid,
                         mod3[l], vec(ln2_g[l]), vec(ln2_b[l]), None if last else mod3[l + 1],
                         layer=l, row_fn=row_fn_for(tm), n_rows=n_rows, n_out_rows=n_rows, tm=tm, tf=tf, seq=seq,
                         ctx_len=ctx_len, n_lat_tiles=n_lat // tm, alpha=alpha)
        if last:
            x_cur = outs[0]
        else:
            x_cur, h = outs
    return x_cur.reshape(batch, seq, d)


def kernel(x, c, ctx, c_ctx, w_mod, b_mod, w_in, q_gain_a, k_gain_a, sink_b, w_fourier, w_out, ln1_g, ln1_b,
           w_up, w_gate, conv_w, conv_b, w_down, ln2_g, ln2_b):
    return _forward(x, c, ctx, c_ctx, w_mod, b_mod, w_in, q_gain_a, k_gain_a, sink_b, w_fourier, w_out,
                    ln1_g, ln1_b, w_up, w_gate, conv_w, conv_b, w_down, ln2_g, ln2_b)
```

```python
import functools
import math

import numpy as np
import jax
import jax.numpy as jnp
from jax import lax
from jax.experimental import pallas as pl
from jax.experimental.pallas import tpu as pltpu

F32 = jnp.float32
BF16 = jnp.bfloat16

HEAD_DIM = 128
A_KV_HEADS = 2
B_KV_HEADS = 2
F_GROUPS = 4
WINDOW = 128
GRID_W = 64
ROPE_BASE = 10000.0
EPS = 1e-6
NEG_INF = -1e30
LOG2_E = math.log2(math.e)
MOD_ROWS = 16
VMEM_LIMIT_BYTES = 56 * 1024 * 1024


def _cparams(*sem):
    return pltpu.CompilerParams(dimension_semantics=sem, vmem_limit_bytes=VMEM_LIMIT_BYTES)


def _dot(a, b):
    return jnp.dot(a, b, preferred_element_type=F32)


def _ln(v):
    mu = jnp.mean(v, axis=-1, keepdims=True)
    d = v - mu
    var = jnp.mean(d * d, axis=-1, keepdims=True)
    return d * lax.rsqrt(var + EPS)


def _post_norm(x, y, gate, lg, lb, alpha):
    return _ln(alpha * x + gate * y) * lg + lb


def _modulate(x, sh, sc):
    return _ln(x) * (1.0 + sc) + sh


def _mod_kernel(c_ref, w_ref, b_ref, o_ref):
    a = c_ref[...]
    a = a * jax.nn.sigmoid(a)
    o_ref[0] = jnp.dot(a, w_ref[0], preferred_element_type=F32,
                       precision=lax.Precision.HIGHEST) + b_ref[0]


def _mod_all(cc, w_mod, b_mod, tn=1024):
    depth, d, n = w_mod.shape
    return pl.pallas_call(
        _mod_kernel,
        grid=(depth, n // tn),
        in_specs=[pl.BlockSpec((MOD_ROWS, d), lambda l, j: (0, 0)),
                  pl.BlockSpec((1, d, tn), lambda l, j: (l, 0, j)),
                  pl.BlockSpec((1, 1, tn), lambda l, j: (l, 0, j))],
        out_specs=pl.BlockSpec((1, MOD_ROWS, tn), lambda l, j: (l, 0, j)),
        out_shape=jax.ShapeDtypeStruct((depth, MOD_ROWS, n), F32),
        compiler_params=_cparams("parallel", "parallel"),
        name="mod_all",
    )(cc, w_mod, b_mod.reshape(depth, 1, n))


def _fold_kernel(cs_ref, w_ref, o_ref):
    o_ref[0] = jnp.dot(cs_ref[...], w_ref[0], preferred_element_type=F32,
                       precision=lax.Precision.HIGHEST)


def _fourier_fold(w_fourier):
    depth, groups, c, _ = w_fourier.shape
    k = np.arange(c)
    ang = 2.0 * np.pi * np.outer(k, k) / c
    cs = np.concatenate([np.cos(ang), -np.sin(ang)], axis=0) / math.sqrt(c)
    out = pl.pallas_call(
        _fold_kernel,
        grid=(depth * groups,),
        in_specs=[pl.BlockSpec((2 * c, c), lambda i: (0, 0)),
                  pl.BlockSpec((1, c, c), lambda i: (i, 0, 0))],
        out_specs=pl.BlockSpec((1, 2 * c, c), lambda i: (i, 0, 0)),
        out_shape=jax.ShapeDtypeStruct((depth * groups, 2 * c, c), F32),
        compiler_params=_cparams("parallel"),
        name="fourier_fold",
    )(jnp.asarray(cs, F32), w_fourier.reshape(depth * groups, c, c))
    out = out.reshape(depth, groups, 2, c, c).transpose(0, 1, 3, 2, 4).reshape(depth, groups, c, 2 * c)
    return out.astype(BF16)


def _ln_mod_kernel(x_ref, c_ref, sh_ref, sc_ref, xa_ref, h_ref, *, n_lat_tiles):
    latent = pl.program_id(0) < n_lat_tiles

    @pl.when(latent)
    def _():
        xa_ref[...] = x_ref[...]

    @pl.when(jnp.logical_not(latent))
    def _():
        xa_ref[...] = c_ref[...]

    h_ref[...] = _modulate(xa_ref[...], sh_ref[0], sc_ref[0]).astype(h_ref.dtype)


def _mod_spec(d, chunk, row_fn):
    return pl.BlockSpec((1, 1, d), lambda i, *_: (row_fn(i), 0, chunk))


def _ln_mod(x_lat, x_ctx, mod3, row_fn, tm):
    n_lat, d = x_lat.shape
    n_ctx = x_ctx.shape[0]
    n_lat_tiles, n = n_lat // tm, n_lat + n_ctx
    return pl.pallas_call(
        functools.partial(_ln_mod_kernel, n_lat_tiles=n_lat_tiles),
        grid=(n // tm,),
        in_specs=[pl.BlockSpec((tm, d), lambda i: (jnp.minimum(i, n_lat_tiles - 1), 0)),
                  pl.BlockSpec((tm, d), lambda i: (jnp.maximum(i - n_lat_tiles, 0), 0)),
                  _mod_spec(d, 0, row_fn), _mod_spec(d, 1, row_fn)],
        out_specs=[pl.BlockSpec((tm, d), lambda i: (i, 0)), pl.BlockSpec((tm, d), lambda i: (i, 0))],
        out_shape=[jax.ShapeDtypeStruct((n, d), F32), jax.ShapeDtypeStruct((n, d), BF16)],
        compiler_params=_cparams("arbitrary"),
        name="ln_mod",
    )(x_lat, x_ctx, mod3, mod3)


def _rope(t, cos, s1, s2):
    return t * cos + pltpu.roll(t, HEAD_DIM - 32, 1) * s1 + pltpu.roll(t, 32, 1) * s2


def _rms(t, gain):
    return t * lax.rsqrt(jnp.mean(t * t, axis=-1, keepdims=True) + EPS) * gain


def _win_kernel(h_ref, w_ref, qg_ref, kg_ref, cos_ref, s1_ref, s2_ref, p_ref, vt_ref, *,
                kinds, tm, tn, scale):
    h = h_ref[...]
    cos, s1, s2 = cos_ref[...], s1_ref[...], s2_ref[...]
    per = tn // HEAD_DIM
    n_v = 0
    for b in range(len(kinds) // per):
        acc = _dot(h, w_ref[:, b * tn:(b + 1) * tn])
        for c in range(per):
            kind = kinds[b * per + c]
            col = (b * per + c) * HEAD_DIM
            t = acc[:, c * HEAD_DIM:(c + 1) * HEAD_DIM]
            if kind == "qa":
                t = _rope(_rms(t, qg_ref[...]), cos, s1, s2) * scale
            elif kind == "qb":
                t = _rope(t, cos, s1, s2) * scale
            elif kind == "ka":
                t = _rope(_rms(t, kg_ref[...]), cos, s1, s2)
            elif kind == "kb":
                t = _rope(t, cos, s1, s2)
            p_ref[:, col:col + HEAD_DIM] = t.astype(p_ref.dtype)
            if kind == "v":
                for r in range(tm // HEAD_DIM):
                    tile = t[r * HEAD_DIM:(r + 1) * HEAD_DIM]
                    vt_ref[r, n_v * HEAD_DIM:(n_v + 1) * HEAD_DIM, :] = tile.T.astype(vt_ref.dtype)
                n_v += 1


def _in_proj(h, w, qg, kg, cos, s1, s2, *, layer, kinds, rope_fn, tm, tn, scale):
    n, d = h.shape
    in_w = w.shape[2]
    n_v = kinds.count("v")
    kern = functools.partial(_win_kernel, kinds=kinds, tm=tm, tn=tn, scale=scale)
    tab = pl.BlockSpec((tm, HEAD_DIM), lambda i: (rope_fn(i), 0))
    vec = pl.BlockSpec((1, HEAD_DIM), lambda i: (0, 0))
    return pl.pallas_call(
        kern,
        grid=(n // tm,),
        in_specs=[pl.BlockSpec((tm, d), lambda i: (i, 0)),
                  pl.BlockSpec((None, d, in_w), lambda i: (layer, 0, 0)),
                  vec, vec, tab, tab, tab],
        out_specs=[pl.BlockSpec((tm, in_w), lambda i: (i, 0)),
                   pl.BlockSpec((tm // HEAD_DIM, n_v * HEAD_DIM, HEAD_DIM), lambda i: (i, 0, 0))],
        out_shape=[jax.ShapeDtypeStruct((n, in_w), BF16),
                   jax.ShapeDtypeStruct((n // HEAD_DIM, n_v * HEAD_DIM, HEAD_DIM), BF16)],
        compiler_params=_cparams("parallel"),
        name="in_proj",
    )(h, w, qg, kg, cos, s1, s2)


def _scores_t(k, q):
    return lax.dot_general(k, q, (((1,), (1,)), ((), ())), preferred_element_type=F32)


def _vt_tile(vt_ref, first, count):
    tiles = [vt_ref[first + t] for t in range(count)]
    return tiles[0] if count == 1 else jnp.concatenate(tiles, axis=1)


def _online_first(s, vt):
    m = jnp.max(s, axis=0, keepdims=True)
    p = jnp.exp2(s - m)
    return m, jnp.sum(p, axis=0, keepdims=True), _dot(vt, p.astype(BF16))


def _exact_zero(acc):
    bits = pltpu.bitcast(acc[0:1, 0:HEAD_DIM], jnp.uint32)
    bits = lax.shift_right_logical(lax.shift_right_logical(bits, jnp.uint32(16)), jnp.uint32(16))
    return pltpu.bitcast(bits, F32).astype(BF16)


SUM_ROWS = 16


def _with_ones(vt):
    return jnp.concatenate([vt, jnp.ones((SUM_ROWS, vt.shape[1]), vt.dtype)], axis=0)


def _online_step(s, vt_ones, m, acc):
    m_new = jnp.maximum(m, jnp.max(s, axis=0, keepdims=True))
    p = jnp.exp2(s - m_new).astype(BF16)
    return m_new, jnp.exp2(m - m_new) * acc + _dot(vt_ones, p)


def _global_attn_kernel(*refs, group, tk, n_chunks, ctx_len):
    if n_chunks:
        q_ref, k_ref, vt_ref, kc_ref, vtc_ref, o_ref = refs[:6]
        s_refs = refs[6:6 + 2 * group]
        acc_refs = refs[6 + 2 * group:]
    else:
        q_ref, kc_ref, vtc_ref, o_ref = refs
    sub = tk // HEAD_DIM
    tq = q_ref.shape[0]
    cols = [slice(g * HEAD_DIM, (g + 1) * HEAD_DIM) for g in range(group)]
    qs = [q_ref[:, c] for c in cols]
    kc, vtc = kc_ref[...], _vt_tile(vtc_ref, 0, ctx_len // HEAD_DIM)
    if not n_chunks:
        for c, q in zip(cols, qs):
            _, l, acc = _online_first(_scores_t(kc, q), vtc)
            o_ref[:, c] = (acc / l).T.astype(o_ref.dtype)
    else:
        assert n_chunks % 2 == 0

        def lat_keys(c):
            return k_ref[pl.ds(pl.multiple_of(c * tk, tk), tk), :]

        def lat_values(c):
            return _with_ones(_vt_tile(vt_ref, c * sub, sub))

        def stage(k_next, slot_next, vt_cur, slot_cur, ms):
            out, gate = [], None
            for g, q in enumerate(qs):
                if k_next is not None:
                    n_keys = k_next.shape[0]
                    s_refs[slot_next * group + g][0:n_keys, :] = _scores_t(
                        k_next if gate is None else k_next + gate, q)
                if vt_cur is not None:
                    n_keys = vt_cur.shape[1]
                    m_new, acc = _online_step(s_refs[slot_cur * group + g][0:n_keys, :], vt_cur, ms[g],
                                              acc_refs[g][...])
                    acc_refs[g][...] = acc
                    gate = _exact_zero(m_new)
                    out.append(m_new)
            return tuple(out)

        def body(i, ms):
            ms = stage(lat_keys(2 * i + 1), 1, lat_values(2 * i), 0, ms)
            return stage(lat_keys(2 * i + 2), 0, lat_values(2 * i + 1), 1, ms)

        for acc_ref in acc_refs:
            acc_ref[...] = jnp.zeros_like(acc_ref)
        ms = (jnp.full((1, tq), -jnp.inf, F32),) * group
        stage(lat_keys(0), 0, None, None, None)
        ms = lax.fori_loop(0, n_chunks // 2 - 1, body, ms)
        ms = stage(lat_keys(n_chunks - 1), 1, lat_values(n_chunks - 2), 0, ms)
        ms = stage(kc, 0, lat_values(n_chunks - 1), 1, ms)
        stage(None, None, _with_ones(vtc), 0, ms)
        for c, acc_ref in zip(cols, acc_refs):
            o = acc_ref[0:HEAD_DIM, :] / acc_ref[HEAD_DIM:HEAD_DIM + 1, :]
            o_ref[:, c] = o.T.astype(o_ref.dtype)


def _window_attn_kernel(*refs, group, tq, seq, ctx_len, local):
    if local:
        sink_ref, bias_ref, q_ref, k_ref, vt_ref, kc_ref, vtc_ref, o_ref = refs
    else:
        sink_ref, q_ref, kc_ref, vtc_ref, o_ref = refs
    kv = pl.program_id(1)
    qi = pl.program_id(2)
    sub = tq // HEAD_DIM
    n_sub = seq // HEAD_DIM
    kc, vtc = kc_ref[...], _vt_tile(vtc_ref, 0, ctx_len // HEAD_DIM)
    if local:
        first = qi * sub
        prev = jnp.maximum(first - 1, 0)
        nxt = jnp.minimum(first + sub, n_sub - 1)
        k_loc = jnp.concatenate(
            [k_ref[pl.ds(pl.multiple_of(prev * HEAD_DIM, HEAD_DIM), HEAD_DIM), :],
             k_ref[pl.ds(pl.multiple_of(first * HEAD_DIM, tq), tq), :],
             k_ref[pl.ds(pl.multiple_of(nxt * HEAD_DIM, HEAD_DIM), HEAD_DIM), :]], axis=0)
        vt_loc = jnp.concatenate(
            [vt_ref[prev]] + [vt_ref[first + t] for t in range(sub)] + [vt_ref[nxt]], axis=1)
        bias = bias_ref[...]

    for g in range(group):
        cols = slice(g * HEAD_DIM, (g + 1) * HEAD_DIM)
        q = q_ref[:, cols]
        sink = sink_ref[kv * group + g] * LOG2_E
        s_c = _scores_t(kc, q)
        m = jnp.maximum(jnp.max(s_c, axis=0, keepdims=True), sink)
        if local:
            s_l = _scores_t(k_loc, q) + bias
            m = jnp.maximum(m, jnp.max(s_l, axis=0, keepdims=True))
        p_c = jnp.exp2(s_c - m)
        denom = jnp.sum(p_c, axis=0, keepdims=True) + jnp.exp2(sink - m)
        acc = _dot(vtc, p_c.astype(BF16))
        if local:
            p_l = jnp.exp2(s_l - m)
            denom = denom + jnp.sum(p_l, axis=0, keepdims=True)
            acc = acc + _dot(vt_loc, p_l.astype(BF16))
        o_ref[:, cols] = (acc / denom).T.astype(o_ref.dtype)


def _window_bias(n_loc, tq):
    key = jnp.arange(n_loc)[:, None] - HEAD_DIM
    band = jnp.abs(key - jnp.arange(tq)[None, :]) <= WINDOW
    variants = [band & (key >= 0), band, band & (key < tq)]
    return jnp.stack([jnp.where(v, 0.0, NEG_INF).astype(F32) for v in variants])


def _attention(p, vt3, out_prev, sink, *, window, latent_queries, batch, seq, ctx_len, q_off, k_off,
               v_row, group, out_width, tq, tk):
    n_all = p.shape[0]
    gw = group * HEAD_DIM
    ctx_blk0 = batch * seq // ctx_len
    kc_spec = pl.BlockSpec((ctx_len, HEAD_DIM), lambda b, kv, qi, *_: (ctx_blk0 + b, k_off // HEAD_DIM + kv))
    vtc_spec = pl.BlockSpec((ctx_len // HEAD_DIM, HEAD_DIM, HEAD_DIM),
                            lambda b, kv, qi, *_: (ctx_blk0 + b, v_row + kv, 0))
    if latent_queries:
        n_q = seq // tq
        q_blk = lambda b, kv, qi, *_: (b * n_q + qi, q_off // gw + kv)
        lat_specs = [pl.BlockSpec((seq, HEAD_DIM), lambda b, kv, qi, *_: (b, k_off // HEAD_DIM + kv)),
                     pl.BlockSpec((seq // HEAD_DIM, HEAD_DIM, HEAD_DIM), lambda b, kv, qi, *_: (b, v_row + kv, 0))]
        lat_args = [p, vt3]
    else:
        tq = ctx_len
        n_q = 1
        q_blk = lambda b, kv, qi, *_: (ctx_blk0 + b, q_off // gw + kv)
        lat_specs, lat_args = [], []
    q_spec = pl.BlockSpec((tq, gw), q_blk)
    o_spec = pl.BlockSpec((tq, gw), lambda b, kv, qi, *_: (q_blk(b, kv, qi)[0], kv))
    if window:
        kern = functools.partial(_window_attn_kernel, group=group, tq=tq, seq=seq, ctx_len=ctx_len,
                                 local=latent_queries)
        pre_specs = [pl.BlockSpec(memory_space=pltpu.SMEM)]
        pre_args = [sink]
        if latent_queries:
            assert n_q >= 2
            n_loc = tq + 2 * HEAD_DIM
            pre_specs.append(pl.BlockSpec(
                (None, n_loc, tq), lambda b, kv, qi: (jnp.where(qi == 0, 0, jnp.where(qi == n_q - 1, 2, 1)), 0, 0)))
            pre_args.append(_window_bias(n_loc, tq))
    else:
        kern = functools.partial(_global_attn_kernel, group=group, tk=tk,
                                 n_chunks=seq // tk if latent_queries else 0, ctx_len=ctx_len)
        pre_specs, pre_args = [], []
    scratch = []
    if latent_queries and not window:
        assert ctx_len <= tk
        scratch = [pltpu.VMEM((tk, tq), F32)] * (2 * group) + [pltpu.VMEM((HEAD_DIM + SUM_ROWS, tq), F32)] * group
    in_specs = pre_specs + [q_spec] + lat_specs + [kc_spec, vtc_spec]
    args = pre_args + [p] + lat_args + [p, vt3]
    aliases = {}
    if out_prev is not None:
        in_specs.append(pl.BlockSpec(memory_space=pl.ANY))
        args.append(out_prev)
        aliases = {len(args) - 1: 0}
        body = kern

        def kern(*refs):
            return body(*refs[:-2], refs[-1])
    return pl.pallas_call(
        kern,
        grid=(batch, A_KV_HEADS, n_q),
        in_specs=in_specs,
        out_specs=o_spec,
        out_shape=jax.ShapeDtypeStruct((n_all, out_width), BF16),
        input_output_aliases=aliases,
        scratch_shapes=scratch,
        compiler_params=_cparams("parallel", "parallel", "arbitrary"),
        name=("window" if window else "global") + ("_lat" if latent_queries else "_ctx"),
    )(*args)


def _chan_kernel(u_ref, w_ref, v_ref):
    groups, c, _ = w_ref.shape
    for g in range(groups):
        cols = slice(g * c, (g + 1) * c)
        r = _dot(u_ref[:, cols], w_ref[g])
        v_ref[0, :, cols] = r[:, :c].astype(v_ref.dtype)
        v_ref[1, :, cols] = r[:, c:].astype(v_ref.dtype)


def _fourier_channels(p, wfold_l, f_off, tm):
    n = p.shape[0]
    groups, c, _ = wfold_l.shape
    f_w = groups * c
    assert f_off % f_w == 0
    return pl.pallas_call(
        _chan_kernel,
        grid=(n // tm,),
        in_specs=[pl.BlockSpec((tm, f_w), lambda i: (i, f_off // f_w)),
                  pl.BlockSpec((groups, c, 2 * c), lambda i: (0, 0, 0))],
        out_specs=pl.BlockSpec((2, tm, f_w), lambda i: (0, i, 0)),
        out_shape=jax.ShapeDtypeStruct((2, n, f_w), BF16),
        compiler_params=_cparams("parallel"),
        name="fourier_channels",
    )(p, wfold_l)


def _dft1_kernel(w_ref, v_ref, y_ref, *, cols):
    x = jnp.concatenate([v_ref[0], v_ref[1]], axis=0)
    y = _dot(w_ref[...], x)
    n1 = y.shape[0] // 2
    for part in range(2):
        for e in range(y.shape[1] // cols):
            y_ref[part, :, e, :] = y[part * n1:(part + 1) * n1, e * cols:(e + 1) * cols]


def _dft2_kernel(m_ref, y_ref, o_ref, *, kb):
    for kk in range(kb):
        x = jnp.concatenate([y_ref[0, kk], y_ref[1, kk]], axis=0).astype(BF16)
        o_ref[:, kk, :] = _dot(m_ref[kk], x)


def _dense_dft_kernel(m_ref, v_ref, prev_ref, o_ref):
    del prev_ref
    x = jnp.concatenate([v_ref[0], v_ref[1]], axis=0)
    o_ref[...] = _dot(m_ref[...], x).astype(o_ref.dtype)


def _dft_tables(seq, ctx_len):
    n1 = n2 = int(round(math.sqrt(seq)))
    assert n1 * n2 == seq
    a1 = 2.0 * np.pi * np.outer(np.arange(n1), np.arange(n1)) / n1
    c1, s1 = np.cos(a1), np.sin(a1)
    w1 = np.block([[c1, s1], [-s1, c1]])
    k = np.arange(n1)[:, None, None] + n1 * np.arange(n2)[None, :, None]
    a2 = 2.0 * np.pi * ((k * np.arange(n2)[None, None, :]) % seq) / seq
    m2 = np.concatenate([np.cos(a2), np.sin(a2)], axis=2) / math.sqrt(seq)
    ac = 2.0 * np.pi * np.outer(np.arange(ctx_len), np.arange(ctx_len)) / ctx_len
    mc = np.concatenate([np.cos(ac), np.sin(ac)], axis=1) / math.sqrt(ctx_len)
    return n1, n2, jnp.asarray(w1, BF16), jnp.asarray(m2, BF16), jnp.asarray(mc, BF16)


def _fourier_positions(v, tables, *, batch, seq, ctx_len, with_ctx, per=8, kb=8):
    n1, n2, w1, m2, mc = tables
    _, n_all, cols = v.shape
    per, kb = min(per, n2), min(kb, n1)
    tn = per * cols
    v_lat = v.reshape(2, n_all // n2, n2 * cols)
    y = pl.pallas_call(
        functools.partial(_dft1_kernel, cols=cols),
        grid=(batch, n2 // per),
        in_specs=[pl.BlockSpec((2 * n1, 2 * n1), lambda b, j: (0, 0)),
                  pl.BlockSpec((2, n1, tn), lambda b, j: (0, b, j))],
        out_specs=pl.BlockSpec((2, n1, per, cols), lambda b, j: (0, b, j, 0)),
        out_shape=jax.ShapeDtypeStruct((2, batch * n1, n2, cols), F32),
        compiler_params=_cparams("parallel", "parallel"),
        name="dft_stage1",
    )(w1, v_lat)
    groups = n1 // kb
    o = pl.pallas_call(
        functools.partial(_dft2_kernel, kb=kb),
        grid=(batch, groups),
        in_specs=[pl.BlockSpec((kb, n2, 2 * n2), lambda b, kg: (kg, 0, 0)),
                  pl.BlockSpec((2, kb, n2, cols), lambda b, kg: (0, b * groups + kg, 0, 0))],
        out_specs=pl.BlockSpec((n2, kb, cols), lambda b, kg: (b, kg, 0)),
        out_shape=jax.ShapeDtypeStruct((n_all // n1, n1, cols), F32),
        compiler_params=_cparams("parallel", "parallel"),
        name="dft_stage2",
    )(m2, y)
    o = o.reshape(n_all, cols)
    if not with_ctx:
        return o
    blk0 = batch * seq // ctx_len
    return pl.pallas_call(
        _dense_dft_kernel,
        grid=(batch,),
        in_specs=[pl.BlockSpec((ctx_len, 2 * ctx_len), lambda b: (0, 0)),
                  pl.BlockSpec((2, ctx_len, cols), lambda b: (0, blk0 + b, 0)),
                  pl.BlockSpec(memory_space=pl.ANY)],
        out_specs=pl.BlockSpec((ctx_len, cols), lambda b: (blk0 + b, 0)),
        out_shape=jax.ShapeDtypeStruct((n_all, cols), F32),
        input_output_aliases={2: 0},
        compiler_params=_cparams("parallel"),
        name="dft_ctx",
    )(mc, v, o)


def _out_proj_kernel(oa_ref, ob_ref, of_ref, w_ref, x_ref, g_ref, lg_ref, lb_ref,
                     sh_ref, sc_ref, xo_ref, ho_ref, y0_ref, y1_ref, *, alpha, n_chunks=4):
    i = pl.program_id(0)

    @pl.when(i == 0)
    def _():
        y1_ref[...] = jnp.zeros_like(y1_ref)

    def step(y_write, y_read):
        tm, d = y_write.shape
        cw, rc = d // n_chunks, tm // n_chunks
        oa, ob, of = oa_ref[...], ob_ref[...], of_ref[...].astype(BF16)
        gate = None
        for c in range(n_chunks):
            ob_c = ob if gate is None else ob + jnp.concatenate([gate] * (ob.shape[1] // HEAD_DIM), axis=1)
            mixed = jnp.concatenate([oa, ob_c, of], axis=1)
            y_write[:, c * cw:(c + 1) * cw] = _dot(mixed, w_ref[:, c * cw:(c + 1) * cw])
            rows = slice(c * rc, (c + 1) * rc)
            xn = _post_norm(x_ref[rows, :], y_read[rows, :], g_ref[0], lg_ref[...], lb_ref[...], alpha)
            xo_ref[rows, :] = xn
            dev = xn - jnp.mean(xn, axis=-1, keepdims=True)
            var = jnp.mean(dev * dev, axis=-1, keepdims=True)
            h = dev * lax.rsqrt(var + EPS) * (1.0 + sc_ref[0]) + sh_ref[0]
            ho_ref[rows, :] = h.astype(ho_ref.dtype)
            gate = _exact_zero(jnp.broadcast_to(jnp.sum(var, axis=0, keepdims=True), (1, HEAD_DIM)))

    @pl.when(i % 2 == 0)
    def _():
        step(y0_ref, y1_ref)

    @pl.when(i % 2 == 1)
    def _():
        step(y1_ref, y0_ref)


def _out_proj(oa, ob, of, w_out, x_all, mod3, lg, lb, *, layer, row_fn, n_rows, tm, alpha):
    n_all, d = x_all.shape
    wa, wb, wf = oa.shape[1], ob.shape[1], of.shape[1]
    assert wa + wb + wf == w_out.shape[1]
    n_tiles = n_rows // tm
    lead = lambda i: (jnp.minimum(i, n_tiles - 1), 0)
    lag = lambda i: (jnp.maximum(i - 1, 0), 0)
    lag_row = lambda i: row_fn(jnp.maximum(i - 1, 0))
    vec = pl.BlockSpec((1, d), lambda i: (0, 0))
    return pl.pallas_call(
        functools.partial(_out_proj_kernel, alpha=alpha),
        grid=(n_tiles + 1,),
        in_specs=[pl.BlockSpec((tm, wa), lead), pl.BlockSpec((tm, wb), lead), pl.BlockSpec((tm, wf), lead),
                  pl.BlockSpec((None,) + w_out.shape[1:], lambda i: (layer, 0, 0)),
                  pl.BlockSpec((tm, d), lag),
                  _mod_spec(d, 2, lag_row), vec, vec, _mod_spec(d, 3, lag_row), _mod_spec(d, 4, lag_row)],
        out_specs=[pl.BlockSpec((tm, d), lag), pl.BlockSpec((tm, d), lag)],
        out_shape=[jax.ShapeDtypeStruct((n_all, d), F32), jax.ShapeDtypeStruct((n_all, d), BF16)],
        scratch_shapes=[pltpu.VMEM((tm, d), F32), pltpu.VMEM((tm, d), F32)],
        compiler_params=_cparams("arbitrary"),
        name="out_proj",
    )(oa, ob, of, w_out, x_all, mod3, lg, lb, mod3, mod3)


HALO = 16


def _ffn_kernel(hp_ref, hm_ref, hn_ref, wu_ref, wg_ref, cw_ref, cb_ref, wd_ref, x_ref, g_ref, lg_ref,
                lb_ref, *rest, tm, n_tiles, n_lat_tiles, seq, ctx_len, alpha, emit_h, n_split=2, n_epi=4):
    if emit_h:
        sh_ref, sc_ref, xo_ref, ho_ref, hext_ref, acc_ref = rest
    else:
        xo_ref, hext_ref, acc_ref = rest
    i = pl.program_id(0)
    j = pl.program_id(1)
    slot = i % 2
    rc = tm // n_epi

    def epilogue_chunk():
        rows = pl.ds(pl.multiple_of(j * rc, rc), rc)
        v = alpha * x_ref[rows, :] + g_ref[0] * acc_ref[1 - slot, rows, :]
        dev = v - jnp.mean(v, axis=-1, keepdims=True)
        var = jnp.mean(dev * dev, axis=-1, keepdims=True)
        xn = dev * lax.rsqrt(var + EPS) * lg_ref[...] + lb_ref[...]
        xo_ref[rows, :] = xn
        if emit_h:
            dev = xn - jnp.mean(xn, axis=-1, keepdims=True)
            var = jnp.mean(dev * dev, axis=-1, keepdims=True)
            h = dev * lax.rsqrt(var + EPS) * (1.0 + sc_ref[0]) + sh_ref[0]
            ho_ref[rows, :] = h.astype(ho_ref.dtype)
        return _exact_zero(jnp.broadcast_to(jnp.sum(var, axis=0, keepdims=True), (1, HEAD_DIM)))

    def main(tie):
        tf = wu_ref.shape[1]
        th = tf // n_split
        period = jnp.where(i < n_lat_tiles, seq, ctx_len)
        pos = (i * tm + lax.broadcasted_iota(jnp.int32, (tm, th), 0)) & (period - 1)
        first_row, last_row = pos == 0, pos == period - 1
        h_ext, h_main = hext_ref[...], hm_ref[...]
        part = None
        for s in range(n_split):
            sl = slice(s * th, (s + 1) * th)
            gext = _dot(h_ext, wg_ref[:, sl])
            u = _dot(h_main, wu_ref[:, sl])
            g_prev = jnp.where(first_row, 0.0, gext[HALO - 1:HALO - 1 + tm])
            g_cur = gext[HALO:HALO + tm]
            g_next = jnp.where(last_row, 0.0, gext[HALO + 1:HALO + 1 + tm])
            cw, cb = cw_ref[:, sl], cb_ref[:, sl]
            if tie is not None and s == 0:
                cb = cb + jnp.concatenate([tie.astype(F32)] * (th // HEAD_DIM), axis=1)
            gc = cb + cw[0:1] * g_prev + cw[1:2] * g_cur + cw[2:3] * g_next
            act = (gc * jax.nn.sigmoid(gc) * u).astype(BF16)
            d_part = _dot(act, wd_ref[sl, :])
            part = d_part if part is None else part + d_part
        acc_ref[slot] += part

    active = i < n_tiles

    @pl.when(jnp.logical_and(active, j == 0))
    def _():
        hext_ref[0:HALO] = hp_ref[...]
        hext_ref[HALO:HALO + tm] = hm_ref[...]
        hext_ref[HALO + tm:] = hn_ref[...]
        acc_ref[slot] = jnp.zeros((tm, acc_ref.shape[2]), F32)

    @pl.when(jnp.logical_and(i == 0, j == 0))
    def _():
        acc_ref[1] = jnp.zeros((tm, acc_ref.shape[2]), F32)

    @pl.when(jnp.logical_and(active, j < n_epi))
    def _():
        main(epilogue_chunk())

    @pl.when(jnp.logical_and(active, j >= n_epi))
    def _():
        main(None)

    @pl.when(jnp.logical_and(jnp.logical_not(active), j < n_epi))
    def _():
        epilogue_chunk()


def _conv_ffn(h, w_up, w_gate, conv_w, conv_b, w_down, x_all, mod3, lg, lb, mod3_next, *, layer, row_fn,
              n_rows, n_out_rows, tm, tf, seq, ctx_len, n_lat_tiles, alpha):
    n_all, d = x_all.shape
    d_ff = w_up.shape[2]
    emit_h = mod3_next is not None
    per = tm // HALO
    last_halo = n_rows // HALO - 1
    n_tiles, nj = n_rows // tm, d_ff // tf
    n_epi = min(4, nj)
    lead = lambda i: jnp.minimum(i, n_tiles - 1)
    lag = lambda i: jnp.maximum(i - 1, 0)
    col = lambda i, j: jnp.where(i < n_tiles, j, nj - 1)
    lag_row = lambda i: row_fn(lag(i))
    vec = pl.BlockSpec((1, d), lambda i, j: (0, 0))
    in_specs = [pl.BlockSpec((HALO, d), lambda i, j: (jnp.maximum(lead(i) * per - 1, 0), 0)),
                pl.BlockSpec((tm, d), lambda i, j: (lead(i), 0)),
                pl.BlockSpec((HALO, d), lambda i, j: (jnp.minimum((lead(i) + 1) * per, last_halo), 0)),
                pl.BlockSpec((None, d, tf), lambda i, j: (layer, 0, col(i, j))),
                pl.BlockSpec((None, d, tf), lambda i, j: (layer, 0, col(i, j))),
                pl.BlockSpec((conv_w.shape[0], tf), lambda i, j: (0, col(i, j))),
                pl.BlockSpec((1, tf), lambda i, j: (0, col(i, j))),
                pl.BlockSpec((None, tf, d), lambda i, j: (layer, col(i, j), 0)),
                pl.BlockSpec((tm, d), lambda i, j: (lag(i), 0)),
                _mod_spec(d, 5, lag_row), vec, vec]
    args = [h, h, h, w_up, w_gate, conv_w, conv_b, w_down, x_all, mod3, lg, lb]
    out_specs = [pl.BlockSpec((tm, d), lambda i, j: (lag(i), 0))]
    out_shape = [jax.ShapeDtypeStruct((n_out_rows, d), F32)]
    if emit_h:
        in_specs += [_mod_spec(d, 0, lag_row), _mod_spec(d, 1, lag_row)]
        args += [mod3_next, mod3_next]
        out_specs.append(pl.BlockSpec((tm, d), lambda i, j: (lag(i), 0)))
        out_shape.append(jax.ShapeDtypeStruct((n_out_rows, d), BF16))
    kern = functools.partial(_ffn_kernel, tm=tm, n_tiles=n_tiles, n_lat_tiles=n_lat_tiles, seq=seq,
                             ctx_len=ctx_len, alpha=alpha, emit_h=emit_h, n_epi=n_epi)
    return pl.pallas_call(
        kern,
        grid=(n_tiles + 1, nj),
        in_specs=in_specs,
        out_specs=out_specs,
        out_shape=out_shape,
        scratch_shapes=[pltpu.VMEM((tm + 2 * HALO, d), BF16), pltpu.VMEM((2, tm, d), F32)],
        compiler_params=_cparams("arbitrary", "arbitrary"),
        name="conv_ffn",
    )(*args)


def _rope_tables(seq, pad_rows):
    rows = seq // GRID_W
    row = jnp.repeat(jnp.arange(rows), GRID_W).astype(F32)
    col = jnp.tile(jnp.arange(GRID_W), rows).astype(F32)
    n_freq = HEAD_DIM // 4
    inv_freq = ROPE_BASE ** (-jnp.arange(n_freq, dtype=F32) / n_freq)
    ar, ac = row[:, None] * inv_freq, col[:, None] * inv_freq
    zero = jnp.zeros_like(ar)
    cos = jnp.concatenate([jnp.cos(ar), jnp.cos(ar), jnp.cos(ac), jnp.cos(ac)], axis=1)
    s1 = jnp.concatenate([-jnp.sin(ar), zero, -jnp.sin(ac), zero], axis=1)
    s2 = jnp.concatenate([zero, jnp.sin(ar), zero, jnp.sin(ac)], axis=1)
    pad = lambda t, v: jnp.concatenate([t, jnp.full((pad_rows, HEAD_DIM), v, F32)], axis=0)
    return pad(cos, 1.0), pad(s1, 0.0), pad(s2, 0.0)


def _forward(x, c, ctx, c_ctx, w_mod, b_mod, w_in, q_gain_a, k_gain_a, sink_b, w_fourier, w_out,
             ln1_g, ln1_b, w_up, w_gate, conv_w, conv_b, w_down, ln2_g, ln2_b,
             *, tm=512, tm_in=512, tm_out=256, tn_in=512, tf=512, tq=512, tk=512):
    batch, seq, d = x.shape
    ctx_len = ctx.shape[1]
    depth = w_mod.shape[0]
    n_lat, n_ctx = batch * seq, batch * ctx_len
    n_all = n_lat + n_ctx
    qa_w, qb_w, f_w = d // 2, d // 4, d // 4
    ka_w, kb_w = A_KV_HEADS * HEAD_DIM, B_KV_HEADS * HEAD_DIM
    assert f_w == F_GROUPS * HEAD_DIM and batch + 1 <= MOD_ROWS
    assert seq & (seq - 1) == 0 and ctx_len & (ctx_len - 1) == 0
    widths = (("qa", qa_w), ("qb", qb_w), ("ka", ka_w), ("v", ka_w), ("kb", kb_w), ("v", kb_w), ("f", f_w))
    kinds = tuple(kind for kind, w in widths for _ in range(w // HEAD_DIM))
    qb_off = qa_w
    ka_off = qa_w + qb_w
    kb_off = ka_off + 2 * ka_w
    f_off = kb_off + 2 * kb_w
    alpha = (2.0 * depth) ** 0.25
    scale = HEAD_DIM ** -0.5 * LOG2_E

    def row_fn_for(t):
        return lambda i: jnp.where(i < n_lat // t, i // (seq // t), batch)

    def rope_fn(i):
        return jnp.where(i < n_lat // tm_in, i % (seq // tm_in), seq // tm_in)

    cc = jnp.concatenate([c, c_ctx[None], jnp.zeros((MOD_ROWS - batch - 1, d), F32)], axis=0)
    mod = _mod_all(cc, w_mod, b_mod)
    wfold = _fourier_fold(w_fourier)
    cos, s1, s2 = _rope_tables(seq, tm_in)
    tables = _dft_tables(seq, ctx_len)
    mod3 = [mod[l].reshape(MOD_ROWS, 1, 6 * d) for l in range(depth)]
    vec = lambda t: t.reshape(1, -1)

    w_in_b, w_out_b = w_in.astype(BF16), w_out.astype(BF16)
    w_up_b, w_gate_b, w_down_b = w_up.astype(BF16), w_gate.astype(BF16), w_down.astype(BF16)

    x_cur, h = _ln_mod(x.reshape(n_lat, d), ctx.reshape(n_ctx, d), mod3[0], row_fn_for(tm), tm)
    for l in range(depth):
        last = l == depth - 1
        p, vt3 = _in_proj(h, w_in_b, vec(q_gain_a[l]), vec(k_gain_a[l]), cos, s1, s2, layer=l, kinds=kinds,
                          rope_fn=rope_fn, tm=tm_in, tn=tn_in, scale=scale)
        common = dict(batch=batch, seq=seq, ctx_len=ctx_len, tq=tq, tk=tk)
        ga = dict(window=False, q_off=0, k_off=ka_off, v_row=0, group=qa_w // ka_w, out_width=qa_w, **common)
        gb = dict(window=True, q_off=qb_off, k_off=kb_off, v_row=A_KV_HEADS, group=qb_w // kb_w,
                  out_width=qb_w, **common)
        o_a = _attention(p, vt3, None, None, latent_queries=True, **ga)
        o_b = _attention(p, vt3, None, sink_b[l], latent_queries=True, **gb)
        v = _fourier_channels(p, wfold[l], f_off, 2 * tm if n_all % (2 * tm) == 0 else tm)
        if not last:
            o_a = _attention(p, vt3, o_a, None, latent_queries=False, **ga)
            o_b = _attention(p, vt3, o_b, sink_b[l], latent_queries=False, **gb)
        o_f = _fourier_positions(v, tables, batch=batch, seq=seq, ctx_len=ctx_len, with_ctx=not last)
        n_rows = n_lat if last else n_all
        x_mid, h_mid = _out_proj(o_a, o_b, o_f, w_out_b, x_cur, mod3[l], vec(ln1_g[l]), vec(ln1_b[l]),
                                 layer=l, row_fn=row_fn_for(tm_out), n_rows=n_rows, tm=tm_out, alpha=alpha)
        outs = _conv_ffn(h_mid, w_up_b, w_gate_b, conv_w[l], vec(conv_b[l]), w_down_b, x_mid,
                         mod3[l], vec(ln2_g[l]), vec(ln2_b[l]), None if last else mod3[l + 1],
                         layer=l, row_fn=row_fn_for(tm), n_rows=n_rows, n_out_rows=n_rows, tm=tm, tf=tf, seq=seq,
                         ctx_len=ctx_len, n_lat_tiles=n_lat // tm, alpha=alpha)
        if last:
            x_cur = outs[0]
        else:
            x_cur, h = outs
    return x_cur.reshape(batch, seq, d)


def kernel(x, c, ctx, c_ctx, w_mod, b_mod, w_in, q_gain_a, k_gain_a, sink_b, w_fourier, w_out, ln1_g, ln1_b,
           w_up, w_gate, conv_w, conv_b, w_down, ln2_g, ln2_b):
    return _forward(x, c, ctx, c_ctx, w_mod, b_mod, w_in, q_gain_a, k_gain_a, sink_b, w_fourier, w_out,
                    ln1_g, ln1_b, w_up, w_gate, conv_w, conv_b, w_down, ln2_g, ln2_b)
```

```python
import functools
import math

import numpy as np
import jax
import jax.numpy as jnp
from jax import lax
from jax.experimental import pallas as pl
from jax.experimental.pallas import tpu as pltpu

F32 = jnp.float32
BF16 = jnp.bfloat16

HEAD_DIM = 128
A_KV_HEADS = 2
B_KV_HEADS = 2
F_GROUPS = 4
WINDOW = 128
GRID_W = 64
ROPE_BASE = 10000.0
EPS = 1e-6
NEG_INF = -1e30
LOG2_E = math.log2(math.e)
MOD_ROWS = 16
VMEM_LIMIT_BYTES = 56 * 1024 * 1024


def _cparams(*sem):
    return pltpu.CompilerParams(dimension_semantics=sem, vmem_limit_bytes=VMEM_LIMIT_BYTES)


def _dot(a, b):
    return jnp.dot(a, b, preferred_element_type=F32)


def _ln(v):
    mu = jnp.mean(v, axis=-1, keepdims=True)
    d = v - mu
    var = jnp.mean(d * d, axis=-1, keepdims=True)
    return d * lax.rsqrt(var + EPS)


def _post_norm(x, y, gate, lg, lb, alpha):
    return _ln(alpha * x + gate * y) * lg + lb


def _modulate(x, sh, sc):
    return _ln(x) * (1.0 + sc) + sh


def _mod_kernel(c_ref, w_ref, b_ref, o_ref):
    a = c_ref[...]
    a = a * jax.nn.sigmoid(a)
    a_hi = a.astype(BF16)
    a_lo = (a - a_hi.astype(F32)).astype(BF16)
    w = w_ref[0]
    w_hi = w.astype(BF16)
    w_lo = (w - w_hi.astype(F32)).astype(BF16)
    both = _dot(jnp.concatenate([a_hi, a_lo], axis=0), w_hi)
    o_ref[0] = both[:MOD_ROWS] + both[MOD_ROWS:] + _dot(a_hi, w_lo) + b_ref[0]


def _mod_all(cc, w_mod, b_mod, tn=1024):
    depth, d, n = w_mod.shape
    return pl.pallas_call(
        _mod_kernel,
        grid=(depth, n // tn),
        in_specs=[pl.BlockSpec((MOD_ROWS, d), lambda l, j: (0, 0)),
                  pl.BlockSpec((1, d, tn), lambda l, j: (l, 0, j)),
                  pl.BlockSpec((1, 1, tn), lambda l, j: (l, 0, j))],
        out_specs=pl.BlockSpec((1, MOD_ROWS, tn), lambda l, j: (l, 0, j)),
        out_shape=jax.ShapeDtypeStruct((depth, MOD_ROWS, n), F32),
        compiler_params=_cparams("parallel", "parallel"),
        name="mod_all",
    )(cc, w_mod, b_mod.reshape(depth, 1, n))


def _fold_kernel(cs_ref, w_ref, o_ref):
    o_ref[0] = jnp.dot(cs_ref[...], w_ref[0], preferred_element_type=F32,
                       precision=lax.Precision.HIGHEST)


def _fourier_fold(w_fourier):
    depth, groups, c, _ = w_fourier.shape
    k = np.arange(c)
    ang = 2.0 * np.pi * np.outer(k, k) / c
    cs = np.concatenate([np.cos(ang), -np.sin(ang)], axis=0) / math.sqrt(c)
    out = pl.pallas_call(
        _fold_kernel,
        grid=(depth * groups,),
        in_specs=[pl.BlockSpec((2 * c, c), lambda i: (0, 0)),
                  pl.BlockSpec((1, c, c), lambda i: (i, 0, 0))],
        out_specs=pl.BlockSpec((1, 2 * c, c), lambda i: (i, 0, 0)),
        out_shape=jax.ShapeDtypeStruct((depth * groups, 2 * c, c), F32),
        compiler_params=_cparams("parallel"),
        name="fourier_fold",
    )(jnp.asarray(cs, F32), w_fourier.reshape(depth * groups, c, c))
    out = out.reshape(depth, groups, 2, c, c).transpose(0, 1, 3, 2, 4).reshape(depth, groups, c, 2 * c)
    return out.astype(BF16)


def _ln_mod_kernel(x_ref, c_ref, sh_ref, sc_ref, h_ref, *, n_lat_tiles):
    latent = pl.program_id(0) < n_lat_tiles

    @pl.when(latent)
    def _():
        h_ref[...] = _modulate(x_ref[...], sh_ref[0], sc_ref[0]).astype(h_ref.dtype)

    @pl.when(jnp.logical_not(latent))
    def _():
        h_ref[...] = _modulate(c_ref[...], sh_ref[0], sc_ref[0]).astype(h_ref.dtype)


def _mod_spec(d, chunk, row_fn):
    return pl.BlockSpec((1, 1, d), lambda i, *_: (row_fn(i), 0, chunk))


def _ln_mod(x_lat, x_ctx, mod3, row_fn, tm):
    n_lat, d = x_lat.shape
    n_ctx = x_ctx.shape[0]
    n_lat_tiles, n = n_lat // tm, n_lat + n_ctx
    return pl.pallas_call(
        functools.partial(_ln_mod_kernel, n_lat_tiles=n_lat_tiles),
        grid=(n // tm,),
        in_specs=[pl.BlockSpec((tm, d), lambda i: (jnp.minimum(i, n_lat_tiles - 1), 0)),
                  pl.BlockSpec((tm, d), lambda i: (jnp.maximum(i - n_lat_tiles, 0), 0)),
                  _mod_spec(d, 0, row_fn), _mod_spec(d, 1, row_fn)],
        out_specs=pl.BlockSpec((tm, d), lambda i: (i, 0)),
        out_shape=jax.ShapeDtypeStruct((n, d), BF16),
        compiler_params=_cparams("parallel"),
        name="ln_mod",
    )(x_lat, x_ctx, mod3, mod3)


def _rope(t, cos, s1, s2):
    return t * cos + pltpu.roll(t, HEAD_DIM - 32, 1) * s1 + pltpu.roll(t, 32, 1) * s2


def _rms(t, gain):
    return t * lax.rsqrt(jnp.mean(t * t, axis=-1, keepdims=True) + EPS) * gain


def _win_kernel(h_ref, w_ref, qg_ref, kg_ref, cos_ref, s1_ref, s2_ref, p_ref, vt_ref, *,
                kinds, tm, tn, scale):
    h = h_ref[...]
    cos, s1, s2 = cos_ref[...], s1_ref[...], s2_ref[...]
    per = tn // HEAD_DIM
    n_v = 0
    for b in range(len(kinds) // per):
        acc = _dot(h, w_ref[:, b * tn:(b + 1) * tn])
        for c in range(per):
            kind = kinds[b * per + c]
            col = (b * per + c) * HEAD_DIM
            t = acc[:, c * HEAD_DIM:(c + 1) * HEAD_DIM]
            if kind == "qa":
                t = _rope(_rms(t, qg_ref[...]), cos, s1, s2) * scale
            elif kind == "qb":
                t = _rope(t, cos, s1, s2) * scale
            elif kind == "ka":
                t = _rope(_rms(t, kg_ref[...]), cos, s1, s2)
            elif kind == "kb":
                t = _rope(t, cos, s1, s2)
            p_ref[:, col:col + HEAD_DIM] = t.astype(p_ref.dtype)
            if kind == "v":
                for r in range(tm // HEAD_DIM):
                    tile = t[r * HEAD_DIM:(r + 1) * HEAD_DIM]
                    vt_ref[r, n_v * HEAD_DIM:(n_v + 1) * HEAD_DIM, :] = tile.T.astype(vt_ref.dtype)
                n_v += 1


def _in_proj(h, w, qg, kg, cos, s1, s2, *, layer, kinds, rope_fn, tm, tn, scale):
    n, d = h.shape
    in_w = w.shape[2]
    n_v = kinds.count("v")
    kern = functools.partial(_win_kernel, kinds=kinds, tm=tm, tn=tn, scale=scale)
    tab = pl.BlockSpec((tm, HEAD_DIM), lambda i: (rope_fn(i), 0))
    vec = pl.BlockSpec((1, HEAD_DIM), lambda i: (0, 0))
    return pl.pallas_call(
        kern,
        grid=(n // tm,),
        in_specs=[pl.BlockSpec((tm, d), lambda i: (i, 0)),
                  pl.BlockSpec((None, d, in_w), lambda i: (layer, 0, 0)),
                  vec, vec, tab, tab, tab],
        out_specs=[pl.BlockSpec((tm, in_w), lambda i: (i, 0)),
                   pl.BlockSpec((tm // HEAD_DIM, n_v * HEAD_DIM, HEAD_DIM), lambda i: (i, 0, 0))],
        out_shape=[jax.ShapeDtypeStruct((n, in_w), BF16),
                   jax.ShapeDtypeStruct((n // HEAD_DIM, n_v * HEAD_DIM, HEAD_DIM), BF16)],
        compiler_params=_cparams("parallel"),
        name="in_proj",
    )(h, w, qg, kg, cos, s1, s2)


def _scores_t(k, q):
    return lax.dot_general(k, q, (((1,), (1,)), ((), ())), preferred_element_type=F32)


def _vt_tile(vt_ref, first, count):
    tiles = [vt_ref[first + t] for t in range(count)]
    return tiles[0] if count == 1 else jnp.concatenate(tiles, axis=1)


def _online_first(s, vt):
    m = jnp.max(s, axis=0, keepdims=True)
    p = jnp.exp2(s - m)
    return m, jnp.sum(p, axis=0, keepdims=True), _dot(vt, p.astype(BF16))


def _exact_zero(acc):
    bits = pltpu.bitcast(acc[0:1, 0:HEAD_DIM], jnp.uint32)
    bits = lax.shift_right_logical(lax.shift_right_logical(bits, jnp.uint32(16)), jnp.uint32(16))
    return pltpu.bitcast(bits, F32).astype(BF16)


SUM_ROWS = 16


def _with_ones(vt):
    return jnp.concatenate([vt, jnp.ones((SUM_ROWS, vt.shape[1]), vt.dtype)], axis=0)


def _online_step(s, vt_ones, m, acc):
    m_new = jnp.maximum(m, jnp.max(s, axis=0, keepdims=True))
    p = jnp.exp2(s - m_new).astype(BF16)
    return m_new, jnp.exp2(m - m_new) * acc + _dot(vt_ones, p)


def _global_attn_kernel(*refs, group, tk, n_chunks, ctx_len):
    if n_chunks:
        q_ref, k_ref, vt_ref, kc_ref, vtc_ref, o_ref = refs[:6]
        s_refs = refs[6:6 + 2 * group]
        acc_refs = refs[6 + 2 * group:]
    else:
        q_ref, kc_ref, vtc_ref, o_ref = refs
    sub = tk // HEAD_DIM
    tq = q_ref.shape[0]
    cols = [slice(g * HEAD_DIM, (g + 1) * HEAD_DIM) for g in range(group)]
    qs = [q_ref[:, c] for c in cols]
    kc, vtc = kc_ref[...], _vt_tile(vtc_ref, 0, ctx_len // HEAD_DIM)
    if not n_chunks:
        for c, q in zip(cols, qs):
            _, l, acc = _online_first(_scores_t(kc, q), vtc)
            o_ref[:, c] = (acc / l).T.astype(o_ref.dtype)
    else:
        assert n_chunks % 2 == 0

        def lat_keys(c):
            return k_ref[pl.ds(pl.multiple_of(c * tk, tk), tk), :]

        def lat_values(c):
            return _with_ones(_vt_tile(vt_ref, c * sub, sub))

        def stage(k_next, slot_next, vt_cur, slot_cur, ms):
            out, gate = [], None
            for g, q in enumerate(qs):
                if k_next is not None:
                    n_keys = k_next.shape[0]
                    s_refs[slot_next * group + g][0:n_keys, :] = _scores_t(
                        k_next if gate is None else k_next + gate, q)
                if vt_cur is not None:
                    n_keys = vt_cur.shape[1]
                    m_new, acc = _online_step(s_refs[slot_cur * group + g][0:n_keys, :], vt_cur, ms[g],
                                              acc_refs[g][...])
                    acc_refs[g][...] = acc
                    gate = _exact_zero(m_new)
                    out.append(m_new)
            return tuple(out)

        def body(i, ms):
            ms = stage(lat_keys(2 * i + 1), 1, lat_values(2 * i), 0, ms)
            return stage(lat_keys(2 * i + 2), 0, lat_values(2 * i + 1), 1, ms)

        for acc_ref in acc_refs:
            acc_ref[...] = jnp.zeros_like(acc_ref)
        ms = (jnp.full((1, tq), -jnp.inf, F32),) * group
        stage(lat_keys(0), 0, None, None, None)
        ms = lax.fori_loop(0, n_chunks // 2 - 1, body, ms)
        ms = stage(lat_keys(n_chunks - 1), 1, lat_values(n_chunks - 2), 0, ms)
        ms = stage(kc, 0, lat_values(n_chunks - 1), 1, ms)
        stage(None, None, _with_ones(vtc), 0, ms)
        for c, acc_ref in zip(cols, acc_refs):
            o = acc_ref[0:HEAD_DIM, :] / acc_ref[HEAD_DIM:HEAD_DIM + 1, :]
            o_ref[:, c] = o.T.astype(o_ref.dtype)


def _window_attn_kernel(*refs, group, tq, seq, ctx_len, local):
    if local:
        sink_ref, bias_ref, q_ref, k_ref, vt_ref, kc_ref, vtc_ref, o_ref = refs
    else:
        sink_ref, q_ref, kc_ref, vtc_ref, o_ref = refs
    kv = pl.program_id(1)
    qi = pl.program_id(2)
    sub = tq // HEAD_DIM
    n_sub = seq // HEAD_DIM
    kc, vtc = kc_ref[...], _vt_tile(vtc_ref, 0, ctx_len // HEAD_DIM)
    if local:
        first = qi * sub
        prev = jnp.maximum(first - 1, 0)
        nxt = jnp.minimum(first + sub, n_sub - 1)
        k_loc = jnp.concatenate(
            [k_ref[pl.ds(pl.multiple_of(prev * HEAD_DIM, HEAD_DIM), HEAD_DIM), :],
             k_ref[pl.ds(pl.multiple_of(first * HEAD_DIM, tq), tq), :],
             k_ref[pl.ds(pl.multiple_of(nxt * HEAD_DIM, HEAD_DIM), HEAD_DIM), :]], axis=0)
        vt_loc = jnp.concatenate(
            [vt_ref[prev]] + [vt_ref[first + t] for t in range(sub)] + [vt_ref[nxt]], axis=1)
        bias = bias_ref[...]

    for g in range(group):
        cols = slice(g * HEAD_DIM, (g + 1) * HEAD_DIM)
        q = q_ref[:, cols]
        sink = sink_ref[kv * group + g] * LOG2_E
        s_c = _scores_t(kc, q)
        m = jnp.maximum(jnp.max(s_c, axis=0, keepdims=True), sink)
        if local:
            s_l = _scores_t(k_loc, q) + bias
            m = jnp.maximum(m, jnp.max(s_l, axis=0, keepdims=True))
        p_c = jnp.exp2(s_c - m)
        denom = jnp.sum(p_c, axis=0, keepdims=True) + jnp.exp2(sink - m)
        acc = _dot(vtc, p_c.astype(BF16))
        if local:
            p_l = jnp.exp2(s_l - m)
            denom = denom + jnp.sum(p_l, axis=0, keepdims=True)
            acc = acc + _dot(vt_loc, p_l.astype(BF16))
        o_ref[:, cols] = (acc / denom).T.astype(o_ref.dtype)


def _window_bias(n_loc, tq):
    key = jnp.arange(n_loc)[:, None] - HEAD_DIM
    band = jnp.abs(key - jnp.arange(tq)[None, :]) <= WINDOW
    variants = [band & (key >= 0), band, band & (key < tq)]
    return jnp.stack([jnp.where(v, 0.0, NEG_INF).astype(F32) for v in variants])


def _attention(p, vt3, out_prev, sink, *, window, latent_queries, batch, seq, ctx_len, q_off, k_off,
               v_row, group, out_width, tq, tk):
    n_all = p.shape[0]
    gw = group * HEAD_DIM
    ctx_blk0 = batch * seq // ctx_len
    kc_spec = pl.BlockSpec((ctx_len, HEAD_DIM), lambda b, kv, qi, *_: (ctx_blk0 + b, k_off // HEAD_DIM + kv))
    vtc_spec = pl.BlockSpec((ctx_len // HEAD_DIM, HEAD_DIM, HEAD_DIM),
                            lambda b, kv, qi, *_: (ctx_blk0 + b, v_row + kv, 0))
    if latent_queries:
        n_q = seq // tq
        q_blk = lambda b, kv, qi, *_: (b * n_q + qi, q_off // gw + kv)
        lat_specs = [pl.BlockSpec((seq, HEAD_DIM), lambda b, kv, qi, *_: (b, k_off // HEAD_DIM + kv)),
                     pl.BlockSpec((seq // HEAD_DIM, HEAD_DIM, HEAD_DIM), lambda b, kv, qi, *_: (b, v_row + kv, 0))]
        lat_args = [p, vt3]
    else:
        tq = ctx_len
        n_q = 1
        q_blk = lambda b, kv, qi, *_: (ctx_blk0 + b, q_off // gw + kv)
        lat_specs, lat_args = [], []
    q_spec = pl.BlockSpec((tq, gw), q_blk)
    o_spec = pl.BlockSpec((tq, gw), lambda b, kv, qi, *_: (q_blk(b, kv, qi)[0], kv))
    if window:
        kern = functools.partial(_window_attn_kernel, group=group, tq=tq, seq=seq, ctx_len=ctx_len,
                                 local=latent_queries)
        pre_specs = [pl.BlockSpec(memory_space=pltpu.SMEM)]
        pre_args = [sink]
        if latent_queries:
            assert n_q >= 2
            n_loc = tq + 2 * HEAD_DIM
            pre_specs.append(pl.BlockSpec(
                (None, n_loc, tq), lambda b, kv, qi: (jnp.where(qi == 0, 0, jnp.where(qi == n_q - 1, 2, 1)), 0, 0)))
            pre_args.append(_window_bias(n_loc, tq))
    else:
        kern = functools.partial(_global_attn_kernel, group=group, tk=tk,
                                 n_chunks=seq // tk if latent_queries else 0, ctx_len=ctx_len)
        pre_specs, pre_args = [], []
    scratch = []
    if latent_queries and not window:
        assert ctx_len <= tk
        scratch = [pltpu.VMEM((tk, tq), F32)] * (2 * group) + [pltpu.VMEM((HEAD_DIM + SUM_ROWS, tq), F32)] * group
    in_specs = pre_specs + [q_spec] + lat_specs + [kc_spec, vtc_spec]
    args = pre_args + [p] + lat_args + [p, vt3]
    aliases = {}
    if out_prev is not None:
        in_specs.append(pl.BlockSpec(memory_space=pl.ANY))
        args.append(out_prev)
        aliases = {len(args) - 1: 0}
        body = kern

        def kern(*refs):
            return body(*refs[:-2], refs[-1])
    return pl.pallas_call(
        kern,
        grid=(batch, A_KV_HEADS, n_q),
        in_specs=in_specs,
        out_specs=o_spec,
        out_shape=jax.ShapeDtypeStruct((n_all, out_width), BF16),
        input_output_aliases=aliases,
        scratch_shapes=scratch,
        compiler_params=_cparams("parallel", "parallel", "arbitrary"),
        name=("window" if window else "global") + ("_lat" if latent_queries else "_ctx"),
    )(*args)


def _chan_kernel(u_ref, w_ref, v_ref):
    groups, c, _ = w_ref.shape
    for g in range(groups):
        cols = slice(g * c, (g + 1) * c)
        r = _dot(u_ref[:, cols], w_ref[g])
        v_ref[0, :, cols] = r[:, :c].astype(v_ref.dtype)
        v_ref[1, :, cols] = r[:, c:].astype(v_ref.dtype)


def _fourier_channels(p, wfold_l, f_off, tm):
    n = p.shape[0]
    groups, c, _ = wfold_l.shape
    f_w = groups * c
    assert f_off % f_w == 0
    return pl.pallas_call(
        _chan_kernel,
        grid=(n // tm,),
        in_specs=[pl.BlockSpec((tm, f_w), lambda i: (i, f_off // f_w)),
                  pl.BlockSpec((groups, c, 2 * c), lambda i: (0, 0, 0))],
        out_specs=pl.BlockSpec((2, tm, f_w), lambda i: (0, i, 0)),
        out_shape=jax.ShapeDtypeStruct((2, n, f_w), BF16),
        compiler_params=_cparams("parallel"),
        name="fourier_channels",
    )(p, wfold_l)


def _dft_kernel(w_ref, m_ref, v_ref, o_ref, y_ref, *, per):
    _, n1, n2, cols = y_ref.shape
    tn = per * cols
    for j in range(n2 // per):
        lanes = slice(j * tn, (j + 1) * tn)
        x = jnp.concatenate([v_ref[0, :, lanes], v_ref[1, :, lanes]], axis=0)
        y = _dot(w_ref[...], x)
        for part in range(2):
            for e in range(per):
                y_ref[part, :, j * per + e, :] = y[part * n1:(part + 1) * n1, e * cols:(e + 1) * cols]
    for k1 in range(n1):
        x = jnp.concatenate([y_ref[0, k1], y_ref[1, k1]], axis=0).astype(BF16)
        o_ref[:, k1, :] = _dot(m_ref[k1], x)


def _dense_dft_kernel(m_ref, v_ref, prev_ref, o_ref):
    del prev_ref
    x = jnp.concatenate([v_ref[0], v_ref[1]], axis=0)
    o_ref[...] = _dot(m_ref[...], x).astype(o_ref.dtype)


def _dft_tables(seq, ctx_len):
    n1 = n2 = int(round(math.sqrt(seq)))
    assert n1 * n2 == seq
    a1 = 2.0 * np.pi * np.outer(np.arange(n1), np.arange(n1)) / n1
    c1, s1 = np.cos(a1), np.sin(a1)
    w1 = np.block([[c1, s1], [-s1, c1]])
    k = np.arange(n1)[:, None, None] + n1 * np.arange(n2)[None, :, None]
    a2 = 2.0 * np.pi * ((k * np.arange(n2)[None, None, :]) % seq) / seq
    m2 = np.concatenate([np.cos(a2), np.sin(a2)], axis=2) / math.sqrt(seq)
    ac = 2.0 * np.pi * np.outer(np.arange(ctx_len), np.arange(ctx_len)) / ctx_len
    mc = np.concatenate([np.cos(ac), np.sin(ac)], axis=1) / math.sqrt(ctx_len)
    return n1, n2, jnp.asarray(w1, BF16), jnp.asarray(m2, BF16), jnp.asarray(mc, BF16)


def _fourier_positions(v, tables, *, batch, seq, ctx_len, with_ctx, per=8):
    n1, n2, w1, m2, mc = tables
    _, n_all, cols = v.shape
    v_lat = v.reshape(2, n_all // n2, n2 * cols)
    o = pl.pallas_call(
        functools.partial(_dft_kernel, per=min(per, n2)),
        grid=(batch,),
        in_specs=[pl.BlockSpec((2 * n1, 2 * n1), lambda b: (0, 0)),
                  pl.BlockSpec((n1, n2, 2 * n2), lambda b: (0, 0, 0)),
                  pl.BlockSpec((2, n1, n2 * cols), lambda b: (0, b, 0))],
        out_specs=pl.BlockSpec((n2, n1, cols), lambda b: (b, 0, 0)),
        out_shape=jax.ShapeDtypeStruct((n_all // n1, n1, cols), F32),
        scratch_shapes=[pltpu.VMEM((2, n1, n2, cols), F32)],
        compiler_params=_cparams("parallel"),
        name="dft_latent",
    )(w1, m2, v_lat)
    o = o.reshape(n_all, cols)
    if not with_ctx:
        return o
    blk0 = batch * seq // ctx_len
    return pl.pallas_call(
        _dense_dft_kernel,
        grid=(batch,),
        in_specs=[pl.BlockSpec((ctx_len, 2 * ctx_len), lambda b: (0, 0)),
                  pl.BlockSpec((2, ctx_len, cols), lambda b: (0, blk0 + b, 0)),
                  pl.BlockSpec(memory_space=pl.ANY)],
        out_specs=pl.BlockSpec((ctx_len, cols), lambda b: (blk0 + b, 0)),
        out_shape=jax.ShapeDtypeStruct((n_all, cols), F32),
        input_output_aliases={2: 0},
        compiler_params=_cparams("parallel"),
        name="dft_ctx",
    )(mc, v, o)


def _out_proj_kernel(oa_ref, ob_ref, of_ref, w_ref, *rest, alpha, n_lat_tiles=None, n_chunks=4):
    i = pl.program_id(0)
    if n_lat_tiles is None:
        x_ref, g_ref, lg_ref, lb_ref, sh_ref, sc_ref, xo_ref, ho_ref, y0_ref, y1_ref = rest
        residual = lambda rows: x_ref[rows, :]
    else:
        xl_ref, xc_ref, g_ref, lg_ref, lb_ref, sh_ref, sc_ref, xo_ref, ho_ref, y0_ref, y1_ref = rest
        latent = jnp.maximum(i - 1, 0) < n_lat_tiles
        residual = lambda rows: jnp.where(latent, xl_ref[rows, :], xc_ref[rows, :])

    @pl.when(i == 0)
    def _():
        y1_ref[...] = jnp.zeros_like(y1_ref)

    def step(y_write, y_read):
        tm, d = y_write.shape
        cw, rc = d // n_chunks, tm // n_chunks
        oa, ob, of = oa_ref[...], ob_ref[...], of_ref[...].astype(BF16)
        gate = None
        for c in range(n_chunks):
            ob_c = ob if gate is None else ob + jnp.concatenate([gate] * (ob.shape[1] // HEAD_DIM), axis=1)
            mixed = jnp.concatenate([oa, ob_c, of], axis=1)
            y_write[:, c * cw:(c + 1) * cw] = _dot(mixed, w_ref[:, c * cw:(c + 1) * cw])
            rows = slice(c * rc, (c + 1) * rc)
            xn = _post_norm(residual(rows), y_read[rows, :], g_ref[0], lg_ref[...], lb_ref[...], alpha)
            xo_ref[rows, :] = xn
            dev = xn - jnp.mean(xn, axis=-1, keepdims=True)
            var = jnp.mean(dev * dev, axis=-1, keepdims=True)
            h = dev * lax.rsqrt(var + EPS) * (1.0 + sc_ref[0]) + sh_ref[0]
            ho_ref[rows, :] = h.astype(ho_ref.dtype)
            gate = _exact_zero(jnp.broadcast_to(jnp.sum(var, axis=0, keepdims=True), (1, HEAD_DIM)))

    @pl.when(i % 2 == 0)
    def _():
        step(y0_ref, y1_ref)

    @pl.when(i % 2 == 1)
    def _():
        step(y1_ref, y0_ref)


def _out_proj(oa, ob, of, w_out, x_res, mod3, lg, lb, *, layer, row_fn, n_rows, tm, alpha):
    n_all = oa.shape[0]
    wa, wb, wf = oa.shape[1], ob.shape[1], of.shape[1]
    d = w_out.shape[2]
    assert wa + wb + wf == w_out.shape[1]
    n_tiles = n_rows // tm
    lead = lambda i: (jnp.minimum(i, n_tiles - 1), 0)
    lag = lambda i: (jnp.maximum(i - 1, 0), 0)
    lag_row = lambda i: row_fn(jnp.maximum(i - 1, 0))
    vec = pl.BlockSpec((1, d), lambda i: (0, 0))
    if isinstance(x_res, tuple):
        n_lat_tiles = x_res[0].shape[0] // tm
        x_args = list(x_res)
        x_specs = [pl.BlockSpec((tm, d), lambda i: (jnp.minimum(lag(i)[0], n_lat_tiles - 1), 0)),
                   pl.BlockSpec((tm, d), lambda i: (jnp.maximum(lag(i)[0] - n_lat_tiles, 0), 0))]
    else:
        n_lat_tiles, x_args, x_specs = None, [x_res], [pl.BlockSpec((tm, d), lag)]
    return pl.pallas_call(
        functools.partial(_out_proj_kernel, alpha=alpha, n_lat_tiles=n_lat_tiles),
        grid=(n_tiles + 1,),
        in_specs=[pl.BlockSpec((tm, wa), lead), pl.BlockSpec((tm, wb), lead), pl.BlockSpec((tm, wf), lead),
                  pl.BlockSpec((None,) + w_out.shape[1:], lambda i: (layer, 0, 0))] + x_specs + [
                  _mod_spec(d, 2, lag_row), vec, vec, _mod_spec(d, 3, lag_row), _mod_spec(d, 4, lag_row)],
        out_specs=[pl.BlockSpec((tm, d), lag), pl.BlockSpec((tm, d), lag)],
        out_shape=[jax.ShapeDtypeStruct((n_all, d), F32), jax.ShapeDtypeStruct((n_all, d), BF16)],
        scratch_shapes=[pltpu.VMEM((tm, d), F32), pltpu.VMEM((tm, d), F32)],
        compiler_params=_cparams("arbitrary"),
        name="out_proj",
    )(oa, ob, of, w_out, *x_args, mod3, lg, lb, mod3, mod3)


HALO = 16


def _ffn_kernel(hp_ref, hm_ref, hn_ref, wu_ref, wg_ref, cw_ref, cb_ref, wd_ref, x_ref, g_ref, lg_ref,
                lb_ref, *rest, tm, n_tiles, n_lat_tiles, seq, ctx_len, alpha, emit_h, n_split=2, n_epi=4):
    if emit_h:
        sh_ref, sc_ref, xo_ref, ho_ref, hext_ref, acc_ref = rest
    else:
        xo_ref, hext_ref, acc_ref = rest
    i = pl.program_id(0)
    j = pl.program_id(1)
    slot = i % 2
    rc = tm // n_epi

    def epilogue_chunk():
        rows = pl.ds(pl.multiple_of(j * rc, rc), rc)
        v = alpha * x_ref[rows, :] + g_ref[0] * acc_ref[1 - slot, rows, :]
        dev = v - jnp.mean(v, axis=-1, keepdims=True)
        var = jnp.mean(dev * dev, axis=-1, keepdims=True)
        xn = dev * lax.rsqrt(var + EPS) * lg_ref[...] + lb_ref[...]
        xo_ref[rows, :] = xn
        if emit_h:
            dev = xn - jnp.mean(xn, axis=-1, keepdims=True)
            var = jnp.mean(dev * dev, axis=-1, keepdims=True)
            h = dev * lax.rsqrt(var + EPS) * (1.0 + sc_ref[0]) + sh_ref[0]
            ho_ref[rows, :] = h.astype(ho_ref.dtype)
        return _exact_zero(jnp.broadcast_to(jnp.sum(var, axis=0, keepdims=True), (1, HEAD_DIM)))

    def main(tie):
        tf = wu_ref.shape[1]
        th = tf // n_split
        period = jnp.where(i < n_lat_tiles, seq, ctx_len)
        pos = (i * tm + lax.broadcasted_iota(jnp.int32, (tm, th), 0)) & (period - 1)
        first_row, last_row = pos == 0, pos == period - 1
        h_ext, h_main = hext_ref[...], hm_ref[...]
        part = None
        for s in range(n_split):
            sl = slice(s * th, (s + 1) * th)
            gext = _dot(h_ext, wg_ref[:, sl])
            u = _dot(h_main, wu_ref[:, sl])
            g_prev = jnp.where(first_row, 0.0, gext[HALO - 1:HALO - 1 + tm])
            g_cur = gext[HALO:HALO + tm]
            g_next = jnp.where(last_row, 0.0, gext[HALO + 1:HALO + 1 + tm])
            cw, cb = cw_ref[:, sl], cb_ref[:, sl]
            if tie is not None and s == 0:
                cb = cb + jnp.concatenate([tie.astype(F32)] * (th // HEAD_DIM), axis=1)
            gc = cb + cw[0:1] * g_prev + cw[1:2] * g_cur + cw[2:3] * g_next
            act = (gc * jax.nn.sigmoid(gc) * u).astype(BF16)
            d_part = _dot(act, wd_ref[sl, :])
            part = d_part if part is None else part + d_part
        acc_ref[slot] += part

    active = i < n_tiles

    @pl.when(jnp.logical_and(active, j == 0))
    def _():
        hext_ref[0:HALO] = hp_ref[...]
        hext_ref[HALO:HALO + tm] = hm_ref[...]
        hext_ref[HALO + tm:] = hn_ref[...]
        acc_ref[slot] = jnp.zeros((tm, acc_ref.shape[2]), F32)

    @pl.when(jnp.logical_and(i == 0, j == 0))
    def _():
        acc_ref[1] = jnp.zeros((tm, acc_ref.shape[2]), F32)

    @pl.when(jnp.logical_and(active, j < n_epi))
    def _():
        main(epilogue_chunk())

    @pl.when(jnp.logical_and(active, j >= n_epi))
    def _():
        main(None)

    @pl.when(jnp.logical_and(jnp.logical_not(active), j < n_epi))
    def _():
        epilogue_chunk()


def _conv_ffn(h, w_up, w_gate, conv_w, conv_b, w_down, x_all, mod3, lg, lb, mod3_next, *, layer, row_fn,
              n_rows, n_out_rows, tm, tf, seq, ctx_len, n_lat_tiles, alpha):
    n_all, d = x_all.shape
    d_ff = w_up.shape[2]
    emit_h = mod3_next is not None
    per = tm // HALO
    last_halo = n_rows // HALO - 1
    n_tiles, nj = n_rows // tm, d_ff // tf
    n_epi = min(4, nj)
    lead = lambda i: jnp.minimum(i, n_tiles - 1)
    lag = lambda i: jnp.maximum(i - 1, 0)
    col = lambda i, j: jnp.where(i < n_tiles, j, nj - 1)
    lag_row = lambda i: row_fn(lag(i))
    vec = pl.BlockSpec((1, d), lambda i, j: (0, 0))
    in_specs = [pl.BlockSpec((HALO, d), lambda i, j: (jnp.maximum(lead(i) * per - 1, 0), 0)),
                pl.BlockSpec((tm, d), lambda i, j: (lead(i), 0)),
                pl.BlockSpec((HALO, d), lambda i, j: (jnp.minimum((lead(i) + 1) * per, last_halo), 0)),
                pl.BlockSpec((None, d, tf), lambda i, j: (layer, 0, col(i, j))),
                pl.BlockSpec((None, d, tf), lambda i, j: (layer, 0, col(i, j))),
                pl.BlockSpec((conv_w.shape[0], tf), lambda i, j: (0, col(i, j))),
                pl.BlockSpec((1, tf), lambda i, j: (0, col(i, j))),
                pl.BlockSpec((None, tf, d), lambda i, j: (layer, col(i, j), 0)),
                pl.BlockSpec((tm, d), lambda i, j: (lag(i), 0)),
                _mod_spec(d, 5, lag_row), vec, vec]
    args = [h, h, h, w_up, w_gate, conv_w, conv_b, w_down, x_all, mod3, lg, lb]
    out_specs = [pl.BlockSpec((tm, d), lambda i, j: (lag(i), 0))]
    out_shape = [jax.ShapeDtypeStruct((n_out_rows, d), F32)]
    if emit_h:
        in_specs += [_mod_spec(d, 0, lag_row), _mod_spec(d, 1, lag_row)]
        args += [mod3_next, mod3_next]
        out_specs.append(pl.BlockSpec((tm, d), lambda i, j: (lag(i), 0)))
        out_shape.append(jax.ShapeDtypeStruct((n_out_rows, d), BF16))
    kern = functools.partial(_ffn_kernel, tm=tm, n_tiles=n_tiles, n_lat_tiles=n_lat_tiles, seq=seq,
                             ctx_len=ctx_len, alpha=alpha, emit_h=emit_h, n_epi=n_epi)
    return pl.pallas_call(
        kern,
        grid=(n_tiles + 1, nj),
        in_specs=in_specs,
        out_specs=out_specs,
        out_shape=out_shape,
        scratch_shapes=[pltpu.VMEM((tm + 2 * HALO, d), BF16), pltpu.VMEM((2, tm, d), F32)],
        compiler_params=_cparams("arbitrary", "arbitrary"),
        name="conv_ffn",
    )(*args)


def _rope_tables(seq, pad_rows):
    rows = seq // GRID_W
    row = jnp.repeat(jnp.arange(rows), GRID_W).astype(F32)
    col = jnp.tile(jnp.arange(GRID_W), rows).astype(F32)
    n_freq = HEAD_DIM // 4
    inv_freq = ROPE_BASE ** (-jnp.arange(n_freq, dtype=F32) / n_freq)
    ar, ac = row[:, None] * inv_freq, col[:, None] * inv_freq
    zero = jnp.zeros_like(ar)
    cos = jnp.concatenate([jnp.cos(ar), jnp.cos(ar), jnp.cos(ac), jnp.cos(ac)], axis=1)
    s1 = jnp.concatenate([-jnp.sin(ar), zero, -jnp.sin(ac), zero], axis=1)
    s2 = jnp.concatenate([zero, jnp.sin(ar), zero, jnp.sin(ac)], axis=1)
    pad = lambda t, v: jnp.concatenate([t, jnp.full((pad_rows, HEAD_DIM), v, F32)], axis=0)
    return pad(cos, 1.0), pad(s1, 0.0), pad(s2, 0.0)


def _forward(x, c, ctx, c_ctx, w_mod, b_mod, w_in, q_gain_a, k_gain_a, sink_b, w_fourier, w_out,
             ln1_g, ln1_b, w_up, w_gate, conv_w, conv_b, w_down, ln2_g, ln2_b,
             *, tm=512, tm_in=512, tm_out=256, tn_in=512, tf=512, tq=512, tk=512):
    batch, seq, d = x.shape
    ctx_len = ctx.shape[1]
    depth = w_mod.shape[0]
    n_lat, n_ctx = batch * seq, batch * ctx_len
    n_all = n_lat + n_ctx
    qa_w, qb_w, f_w = d // 2, d // 4, d // 4
    ka_w, kb_w = A_KV_HEADS * HEAD_DIM, B_KV_HEADS * HEAD_DIM
    assert f_w == F_GROUPS * HEAD_DIM and batch + 1 <= MOD_ROWS
    assert seq & (seq - 1) == 0 and ctx_len & (ctx_len - 1) == 0
    widths = (("qa", qa_w), ("qb", qb_w), ("ka", ka_w), ("v", ka_w), ("kb", kb_w), ("v", kb_w), ("f", f_w))
    kinds = tuple(kind for kind, w in widths for _ in range(w // HEAD_DIM))
    qb_off = qa_w
    ka_off = qa_w + qb_w
    kb_off = ka_off + 2 * ka_w
    f_off = kb_off + 2 * kb_w
    alpha = (2.0 * depth) ** 0.25
    scale = HEAD_DIM ** -0.5 * LOG2_E

    def row_fn_for(t):
        return lambda i: jnp.where(i < n_lat // t, i // (seq // t), batch)

    def rope_fn(i):
        return jnp.where(i < n_lat // tm_in, i % (seq // tm_in), seq // tm_in)

    cc = jnp.concatenate([c, c_ctx[None], jnp.zeros((MOD_ROWS - batch - 1, d), F32)], axis=0)
    mod = _mod_all(cc, w_mod, b_mod)
    wfold = _fourier_fold(w_fourier)
    cos, s1, s2 = _rope_tables(seq, tm_in)
    tables = _dft_tables(seq, ctx_len)
    mod3 = [mod[l].reshape(MOD_ROWS, 1, 6 * d) for l in range(depth)]
    vec = lambda t: t.reshape(1, -1)

    w_in_b, w_out_b = w_in.astype(BF16), w_out.astype(BF16)
    w_up_b, w_gate_b, w_down_b = w_up.astype(BF16), w_gate.astype(BF16), w_down.astype(BF16)

    x_cur = (x.reshape(n_lat, d), ctx.reshape(n_ctx, d))
    h = _ln_mod(*x_cur, mod3[0], row_fn_for(tm), tm)
    for l in range(depth):
        last = l == depth - 1
        p, vt3 = _in_proj(h, w_in_b, vec(q_gain_a[l]), vec(k_gain_a[l]), cos, s1, s2, layer=l, kinds=kinds,
                          rope_fn=rope_fn, tm=tm_in, tn=tn_in, scale=scale)
        common = dict(batch=batch, seq=seq, ctx_len=ctx_len, tq=tq, tk=tk)
        ga = dict(window=False, q_off=0, k_off=ka_off, v_row=0, group=qa_w // ka_w, out_width=qa_w, **common)
        gb = dict(window=True, q_off=qb_off, k_off=kb_off, v_row=A_KV_HEADS, group=qb_w // kb_w,
                  out_width=qb_w, **common)
        o_a = _attention(p, vt3, None, None, latent_queries=True, **ga)
        o_b = _attention(p, vt3, None, sink_b[l], latent_queries=True, **gb)
        v = _fourier_channels(p, wfold[l], f_off, 2 * tm if n_all % (2 * tm) == 0 else tm)
        if not last:
            o_a = _attention(p, vt3, o_a, None, latent_queries=False, **ga)
            o_b = _attention(p, vt3, o_b, sink_b[l], latent_queries=False, **gb)
        o_f = _fourier_positions(v, tables, batch=batch, seq=seq, ctx_len=ctx_len, with_ctx=not last)
        n_rows = n_lat if last else n_all
        x_mid, h_mid = _out_proj(o_a, o_b, o_f, w_out_b, x_cur, mod3[l], vec(ln1_g[l]), vec(ln1_b[l]),
                                 layer=l, row_fn=row_fn_for(tm_out), n_rows=n_rows, tm=tm_out, alpha=alpha)
        outs = _conv_ffn(h_mid, w_up_b, w_gate_b, conv_w[l], vec(conv_b[l]), w_down_b, x_mid,
                         mod3[l], vec(ln2_g[l]), vec(ln2_b[l]), None if last else mod3[l + 1],
                         layer=l, row_fn=row_fn_for(tm), n_rows=n_rows, n_out_rows=n_rows, tm=tm, tf=tf, seq=seq,
                         ctx_len=ctx_len, n_lat_tiles=n_lat // tm, alpha=alpha)
        if last:
            x_cur = outs[0]
        else:
            x_cur, h = outs
    return x_cur.reshape(batch, seq, d)


def kernel(x, c, ctx, c_ctx, w_mod, b_mod, w_in, q_gain_a, k_gain_a, sink_b, w_fourier, w_out, ln1_g, ln1_b,
           w_up, w_gate, conv_w, conv_b, w_down, ln2_g, ln2_b):
    return _forward(x, c, ctx, c_ctx, w_mod, b_mod, w_in, q_gain_a, k_gain_a, sink_b, w_fourier, w_out,
                    ln1_g, ln1_b, w_up, w_gate, conv_w, conv_b, w_down, ln2_g, ln2_b)
```

```python
import functools
import math

import numpy as np
import jax
import jax.numpy as jnp
from jax import lax
from jax.experimental import pallas as pl
from jax.experimental.pallas import tpu as pltpu

F32 = jnp.float32
BF16 = jnp.bfloat16

HEAD_DIM = 128
A_KV_HEADS = 2
B_KV_HEADS = 2
F_GROUPS = 4
WINDOW = 128
GRID_W = 64
ROPE_BASE = 10000.0
EPS = 1e-6
NEG_INF = -1e30
LOG2_E = math.log2(math.e)
MOD_ROWS = 16
VMEM_LIMIT_BYTES = 56 * 1024 * 1024


def _cparams(*sem):
    return pltpu.CompilerParams(dimension_semantics=sem, vmem_limit_bytes=VMEM_LIMIT_BYTES)


def _dot(a, b):
    return jnp.dot(a, b, preferred_element_type=F32)


def _ln(v):
    mu = jnp.mean(v, axis=-1, keepdims=True)
    d = v - mu
    var = jnp.mean(d * d, axis=-1, keepdims=True)
    return d * lax.rsqrt(var + EPS)


def _post_norm(x, y, gate, lg, lb, alpha):
    return _ln(alpha * x + gate * y) * lg + lb


def _modulate(x, sh, sc):
    return _ln(x) * (1.0 + sc) + sh


def _mod_kernel(c_ref, w_ref, b_ref, o_ref):
    a = c_ref[...]
    a = a * jax.nn.sigmoid(a)
    a_hi = a.astype(BF16)
    a_lo = (a - a_hi.astype(F32)).astype(BF16)
    w = w_ref[0]
    w_hi = w.astype(BF16)
    w_lo = (w - w_hi.astype(F32)).astype(BF16)
    both = _dot(jnp.concatenate([a_hi, a_lo], axis=0), w_hi)
    o_ref[0] = both[:MOD_ROWS] + both[MOD_ROWS:] + _dot(a_hi, w_lo) + b_ref[0]


def _mod_all(cc, w_mod, b_mod, tn=1024):
    depth, d, n = w_mod.shape
    return pl.pallas_call(
        _mod_kernel,
        grid=(depth, n // tn),
        in_specs=[pl.BlockSpec((MOD_ROWS, d), lambda l, j: (0, 0)),
                  pl.BlockSpec((1, d, tn), lambda l, j: (l, 0, j)),
                  pl.BlockSpec((1, 1, tn), lambda l, j: (l, 0, j))],
        out_specs=pl.BlockSpec((1, MOD_ROWS, tn), lambda l, j: (l, 0, j)),
        out_shape=jax.ShapeDtypeStruct((depth, MOD_ROWS, n), F32),
        compiler_params=_cparams("parallel", "parallel"),
        name="mod_all",
    )(cc, w_mod, b_mod.reshape(depth, 1, n))


def _fold_kernel(cs_ref, w_ref, o_ref):
    o_ref[0] = jnp.dot(cs_ref[...], w_ref[0], preferred_element_type=F32,
                       precision=lax.Precision.HIGHEST)


def _fourier_fold(w_fourier):
    depth, groups, c, _ = w_fourier.shape
    k = np.arange(c)
    ang = 2.0 * np.pi * np.outer(k, k) / c
    cs = np.concatenate([np.cos(ang), -np.sin(ang)], axis=0) / math.sqrt(c)
    out = pl.pallas_call(
        _fold_kernel,
        grid=(depth * groups,),
        in_specs=[pl.BlockSpec((2 * c, c), lambda i: (0, 0)),
                  pl.BlockSpec((1, c, c), lambda i: (i, 0, 0))],
        out_specs=pl.BlockSpec((1, 2 * c, c), lambda i: (i, 0, 0)),
        out_shape=jax.ShapeDtypeStruct((depth * groups, 2 * c, c), F32),
        compiler_params=_cparams("parallel"),
        name="fourier_fold",
    )(jnp.asarray(cs, F32), w_fourier.reshape(depth * groups, c, c))
    out = out.reshape(depth, groups, 2, c, c).transpose(0, 1, 3, 2, 4).reshape(depth, groups, c, 2 * c)
    return out.astype(BF16)


def _ln_mod_kernel(x_ref, c_ref, sh_ref, sc_ref, h_ref, *, n_lat_tiles):
    latent = pl.program_id(0) < n_lat_tiles

    @pl.when(latent)
    def _():
        h_ref[...] = _modulate(x_ref[...], sh_ref[0], sc_ref[0]).astype(h_ref.dtype)

    @pl.when(jnp.logical_not(latent))
    def _():
        h_ref[...] = _modulate(c_ref[...], sh_ref[0], sc_ref[0]).astype(h_ref.dtype)


def _mod_spec(d, chunk, row_fn):
    return pl.BlockSpec((1, 1, d), lambda i, *_: (row_fn(i), 0, chunk))


def _ln_mod(x_lat, x_ctx, mod3, row_fn, tm):
    n_lat, d = x_lat.shape
    n_ctx = x_ctx.shape[0]
    n_lat_tiles, n = n_lat // tm, n_lat + n_ctx
    return pl.pallas_call(
        functools.partial(_ln_mod_kernel, n_lat_tiles=n_lat_tiles),
        grid=(n // tm,),
        in_specs=[pl.BlockSpec((tm, d), lambda i: (jnp.minimum(i, n_lat_tiles - 1), 0)),
                  pl.BlockSpec((tm, d), lambda i: (jnp.maximum(i - n_lat_tiles, 0), 0)),
                  _mod_spec(d, 0, row_fn), _mod_spec(d, 1, row_fn)],
        out_specs=pl.BlockSpec((tm, d), lambda i: (i, 0)),
        out_shape=jax.ShapeDtypeStruct((n, d), BF16),
        compiler_params=_cparams("parallel"),
        name="ln_mod",
    )(x_lat, x_ctx, mod3, mod3)


def _rope(t, cos, s1, s2):
    return t * cos + pltpu.roll(t, HEAD_DIM - 32, 1) * s1 + pltpu.roll(t, 32, 1) * s2


def _rms(t, gain):
    return t * lax.rsqrt(jnp.mean(t * t, axis=-1, keepdims=True) + EPS) * gain


def _win_kernel(h_ref, w_ref, qg_ref, kg_ref, cos_ref, s1_ref, s2_ref, p_ref, vt_ref, *,
                kinds, tm, tn, scale):
    h = h_ref[...]
    cos, s1, s2 = cos_ref[...], s1_ref[...], s2_ref[...]
    per = tn // HEAD_DIM
    n_v = 0
    for b in range(len(kinds) // per):
        acc = _dot(h, w_ref[:, b * tn:(b + 1) * tn])
        for c in range(per):
            kind = kinds[b * per + c]
            col = (b * per + c) * HEAD_DIM
            t = acc[:, c * HEAD_DIM:(c + 1) * HEAD_DIM]
            if kind == "qa":
                t = _rope(_rms(t, qg_ref[...]), cos, s1, s2) * scale
            elif kind == "qb":
                t = _rope(t, cos, s1, s2) * scale
            elif kind == "ka":
                t = _rope(_rms(t, kg_ref[...]), cos, s1, s2)
            elif kind == "kb":
                t = _rope(t, cos, s1, s2)
            p_ref[:, col:col + HEAD_DIM] = t.astype(p_ref.dtype)
            if kind == "v":
                for r in range(tm // HEAD_DIM):
                    tile = t[r * HEAD_DIM:(r + 1) * HEAD_DIM]
                    vt_ref[r, n_v * HEAD_DIM:(n_v + 1) * HEAD_DIM, :] = tile.T.astype(vt_ref.dtype)
                n_v += 1


def _in_proj(h, w, qg, kg, cos, s1, s2, *, layer, kinds, rope_fn, tm, tn, scale):
    n, d = h.shape
    in_w = w.shape[2]
    n_v = kinds.count("v")
    kern = functools.partial(_win_kernel, kinds=kinds, tm=tm, tn=tn, scale=scale)
    tab = pl.BlockSpec((tm, HEAD_DIM), lambda i: (rope_fn(i), 0))
    vec = pl.BlockSpec((1, HEAD_DIM), lambda i: (0, 0))
    return pl.pallas_call(
        kern,
        grid=(n // tm,),
        in_specs=[pl.BlockSpec((tm, d), lambda i: (i, 0)),
                  pl.BlockSpec((None, d, in_w), lambda i: (layer, 0, 0)),
                  vec, vec, tab, tab, tab],
        out_specs=[pl.BlockSpec((tm, in_w), lambda i: (i, 0)),
                   pl.BlockSpec((tm // HEAD_DIM, n_v * HEAD_DIM, HEAD_DIM), lambda i: (i, 0, 0))],
        out_shape=[jax.ShapeDtypeStruct((n, in_w), BF16),
                   jax.ShapeDtypeStruct((n // HEAD_DIM, n_v * HEAD_DIM, HEAD_DIM), BF16)],
        compiler_params=_cparams("parallel"),
        name="in_proj",
    )(h, w, qg, kg, cos, s1, s2)


def _scores_t(k, q):
    return lax.dot_general(k, q, (((1,), (1,)), ((), ())), preferred_element_type=F32)


def _vt_tile(vt_ref, first, count):
    tiles = [vt_ref[first + t] for t in range(count)]
    return tiles[0] if count == 1 else jnp.concatenate(tiles, axis=1)


def _online_first(s, vt):
    m = jnp.max(s, axis=0, keepdims=True)
    p = jnp.exp2(s - m)
    return m, jnp.sum(p, axis=0, keepdims=True), _dot(vt, p.astype(BF16))


def _exact_zero(acc):
    bits = pltpu.bitcast(acc[0:1, 0:HEAD_DIM], jnp.uint32)
    bits = lax.shift_right_logical(lax.shift_right_logical(bits, jnp.uint32(16)), jnp.uint32(16))
    return pltpu.bitcast(bits, F32).astype(BF16)


SUM_ROWS = 16


def _with_ones(vt):
    return jnp.concatenate([vt, jnp.ones((SUM_ROWS, vt.shape[1]), vt.dtype)], axis=0)


def _online_step(s, vt_ones, m, acc):
    m_new = jnp.maximum(m, jnp.max(s, axis=0, keepdims=True))
    p = jnp.exp2(s - m_new).astype(BF16)
    return m_new, jnp.exp2(m - m_new) * acc + _dot(vt_ones, p)


def _global_attn_kernel(*refs, group, tk, n_chunks, ctx_len):
    if n_chunks:
        q_ref, k_ref, vt_ref, kc_ref, vtc_ref, o_ref = refs[:6]
        s_refs = refs[6:6 + 2 * group]
        acc_refs = refs[6 + 2 * group:]
    else:
        q_ref, kc_ref, vtc_ref, o_ref = refs
    sub = tk // HEAD_DIM
    tq = q_ref.shape[0]
    cols = [slice(g * HEAD_DIM, (g + 1) * HEAD_DIM) for g in range(group)]
    qs = [q_ref[:, c] for c in cols]
    kc, vtc = kc_ref[...], _vt_tile(vtc_ref, 0, ctx_len // HEAD_DIM)
    if not n_chunks:
        for c, q in zip(cols, qs):
            _, l, acc = _online_first(_scores_t(kc, q), vtc)
            o_ref[:, c] = (acc / l).T.astype(o_ref.dtype)
    else:
        assert n_chunks % 2 == 0

        def lat_keys(c):
            return k_ref[pl.ds(pl.multiple_of(c * tk, tk), tk), :]

        def lat_values(c):
            return _with_ones(_vt_tile(vt_ref, c * sub, sub))

        def stage(k_next, slot_next, vt_cur, slot_cur, ms):
            out, gate = [], None
            for g, q in enumerate(qs):
                if k_next is not None:
                    n_keys = k_next.shape[0]
                    s_refs[slot_next * group + g][0:n_keys, :] = _scores_t(
                        k_next if gate is None else k_next + gate, q)
                if vt_cur is not None:
                    n_keys = vt_cur.shape[1]
                    m_new, acc = _online_step(s_refs[slot_cur * group + g][0:n_keys, :], vt_cur, ms[g],
                                              acc_refs[g][...])
                    acc_refs[g][...] = acc
                    gate = _exact_zero(m_new)
                    out.append(m_new)
            return tuple(out)

        def body(i, ms):
            ms = stage(lat_keys(2 * i + 1), 1, lat_values(2 * i), 0, ms)
            return stage(lat_keys(2 * i + 2), 0, lat_values(2 * i + 1), 1, ms)

        for acc_ref in acc_refs:
            acc_ref[...] = jnp.zeros_like(acc_ref)
        ms = (jnp.full((1, tq), -jnp.inf, F32),) * group
        stage(lat_keys(0), 0, None, None, None)
        ms = lax.fori_loop(0, n_chunks // 2 - 1, body, ms)
        ms = stage(lat_keys(n_chunks - 1), 1, lat_values(n_chunks - 2), 0, ms)
        ms = stage(kc, 0, lat_values(n_chunks - 1), 1, ms)
        stage(None, None, _with_ones(vtc), 0, ms)
        for c, acc_ref in zip(cols, acc_refs):
            o = acc_ref[0:HEAD_DIM, :] / acc_ref[HEAD_DIM:HEAD_DIM + 1, :]
            o_ref[:, c] = o.T.astype(o_ref.dtype)


def _window_attn_kernel(*refs, group, tq, seq, ctx_len, local):
    if local:
        sink_ref, bias_ref, q_ref, k_ref, vt_ref, kc_ref, vtc_ref, o_ref = refs
    else:
        sink_ref, q_ref, kc_ref, vtc_ref, o_ref = refs
    kv = pl.program_id(1)
    qi = pl.program_id(2)
    sub = tq // HEAD_DIM
    n_sub = seq // HEAD_DIM
    kc, vtc = kc_ref[...], _vt_tile(vtc_ref, 0, ctx_len // HEAD_DIM)
    if local:
        first = qi * sub
        prev = jnp.maximum(first - 1, 0)
        nxt = jnp.minimum(first + sub, n_sub - 1)
        k_loc = jnp.concatenate(
            [k_ref[pl.ds(pl.multiple_of(prev * HEAD_DIM, HEAD_DIM), HEAD_DIM), :],
             k_ref[pl.ds(pl.multiple_of(first * HEAD_DIM, tq), tq), :],
             k_ref[pl.ds(pl.multiple_of(nxt * HEAD_DIM, HEAD_DIM), HEAD_DIM), :]], axis=0)
        vt_loc = jnp.concatenate(
            [vt_ref[prev]] + [vt_ref[first + t] for t in range(sub)] + [vt_ref[nxt]], axis=1)
        bias = bias_ref[...]

    for g in range(group):
        cols = slice(g * HEAD_DIM, (g + 1) * HEAD_DIM)
        q = q_ref[:, cols]
        sink = sink_ref[kv * group + g] * LOG2_E
        s_c = _scores_t(kc, q)
        m = jnp.maximum(jnp.max(s_c, axis=0, keepdims=True), sink)
        if local:
            s_l = _scores_t(k_loc, q) + bias
            m = jnp.maximum(m, jnp.max(s_l, axis=0, keepdims=True))
        p_c = jnp.exp2(s_c - m)
        denom = jnp.sum(p_c, axis=0, keepdims=True) + jnp.exp2(sink - m)
        acc = _dot(vtc, p_c.astype(BF16))
        if local:
            p_l = jnp.exp2(s_l - m)
            denom = denom + jnp.sum(p_l, axis=0, keepdims=True)
            acc = acc + _dot(vt_loc, p_l.astype(BF16))
        o_ref[:, cols] = (acc / denom).T.astype(o_ref.dtype)


def _window_bias(n_loc, tq):
    key = jnp.arange(n_loc)[:, None] - HEAD_DIM
    band = jnp.abs(key - jnp.arange(tq)[None, :]) <= WINDOW
    variants = [band & (key >= 0), band, band & (key < tq)]
    return jnp.stack([jnp.where(v, 0.0, NEG_INF).astype(F32) for v in variants])


def _attention(p, vt3, out_prev, sink, *, window, latent_queries, batch, seq, ctx_len, q_off, k_off,
               v_row, group, out_width, tq, tk):
    n_all = p.shape[0]
    gw = group * HEAD_DIM
    ctx_blk0 = batch * seq // ctx_len
    kc_spec = pl.BlockSpec((ctx_len, HEAD_DIM), lambda b, kv, qi, *_: (ctx_blk0 + b, k_off // HEAD_DIM + kv))
    vtc_spec = pl.BlockSpec((ctx_len // HEAD_DIM, HEAD_DIM, HEAD_DIM),
                            lambda b, kv, qi, *_: (ctx_blk0 + b, v_row + kv, 0))
    if latent_queries:
        n_q = seq // tq
        q_blk = lambda b, kv, qi, *_: (b * n_q + qi, q_off // gw + kv)
        lat_specs = [pl.BlockSpec((seq, HEAD_DIM), lambda b, kv, qi, *_: (b, k_off // HEAD_DIM + kv)),
                     pl.BlockSpec((seq // HEAD_DIM, HEAD_DIM, HEAD_DIM), lambda b, kv, qi, *_: (b, v_row + kv, 0))]
        lat_args = [p, vt3]
    else:
        tq = ctx_len
        n_q = 1
        q_blk = lambda b, kv, qi, *_: (ctx_blk0 + b, q_off // gw + kv)
        lat_specs, lat_args = [], []
    q_spec = pl.BlockSpec((tq, gw), q_blk)
    o_spec = pl.BlockSpec((tq, gw), lambda b, kv, qi, *_: (q_blk(b, kv, qi)[0], kv))
    if window:
        kern = functools.partial(_window_attn_kernel, group=group, tq=tq, seq=seq, ctx_len=ctx_len,
                                 local=latent_queries)
        pre_specs = [pl.BlockSpec(memory_space=pltpu.SMEM)]
        pre_args = [sink]
        if latent_queries:
            assert n_q >= 2
            n_loc = tq + 2 * HEAD_DIM
            pre_specs.append(pl.BlockSpec(
                (None, n_loc, tq), lambda b, kv, qi: (jnp.where(qi == 0, 0, jnp.where(qi == n_q - 1, 2, 1)), 0, 0)))
            pre_args.append(_window_bias(n_loc, tq))
    else:
        kern = functools.partial(_global_attn_kernel, group=group, tk=tk,
                                 n_chunks=seq // tk if latent_queries else 0, ctx_len=ctx_len)
        pre_specs, pre_args = [], []
    scratch = []
    if latent_queries and not window:
        assert ctx_len <= tk
        scratch = [pltpu.VMEM((tk, tq), F32)] * (2 * group) + [pltpu.VMEM((HEAD_DIM + SUM_ROWS, tq), F32)] * group
    in_specs = pre_specs + [q_spec] + lat_specs + [kc_spec, vtc_spec]
    args = pre_args + [p] + lat_args + [p, vt3]
    aliases = {}
    if out_prev is not None:
        in_specs.append(pl.BlockSpec(memory_space=pl.ANY))
        args.append(out_prev)
        aliases = {len(args) - 1: 0}
        body = kern

        def kern(*refs):
            return body(*refs[:-2], refs[-1])
    return pl.pallas_call(
        kern,
        grid=(batch, A_KV_HEADS, n_q),
        in_specs=in_specs,
        out_specs=o_spec,
        out_shape=jax.ShapeDtypeStruct((n_all, out_width), BF16),
        input_output_aliases=aliases,
        scratch_shapes=scratch,
        compiler_params=_cparams("parallel", "parallel", "arbitrary"),
        name=("window" if window else "global") + ("_lat" if latent_queries else "_ctx"),
    )(*args)


def _chan_kernel(u_ref, w_ref, v_ref):
    groups, c, _ = w_ref.shape
    for g in range(groups):
        cols = slice(g * c, (g + 1) * c)
        r = _dot(u_ref[:, cols], w_ref[g])
        v_ref[0, :, cols] = r[:, :c].astype(v_ref.dtype)
        v_ref[1, :, cols] = r[:, c:].astype(v_ref.dtype)


def _fourier_channels(p, wfold_l, f_off, tm):
    n = p.shape[0]
    groups, c, _ = wfold_l.shape
    f_w = groups * c
    assert f_off % f_w == 0
    return pl.pallas_call(
        _chan_kernel,
        grid=(n // tm,),
        in_specs=[pl.BlockSpec((tm, f_w), lambda i: (i, f_off // f_w)),
                  pl.BlockSpec((groups, c, 2 * c), lambda i: (0, 0, 0))],
        out_specs=pl.BlockSpec((2, tm, f_w), lambda i: (0, i, 0)),
        out_shape=jax.ShapeDtypeStruct((2, n, f_w), BF16),
        compiler_params=_cparams("parallel"),
        name="fourier_channels",
    )(p, wfold_l)


def _dft_kernel(w_ref, m_ref, v_ref, o_ref, y_ref, *, per):
    _, n1, n2, cols = y_ref.shape
    tn = per * cols
    for j in range(n2 // per):
        lanes = slice(j * tn, (j + 1) * tn)
        x = jnp.concatenate([v_ref[0, :, lanes], v_ref[1, :, lanes]], axis=0)
        y = _dot(w_ref[...], x)
        for part in range(2):
            for e in range(per):
                y_ref[part, :, j * per + e, :] = y[part * n1:(part + 1) * n1, e * cols:(e + 1) * cols]
    for k1 in range(n1):
        x = jnp.concatenate([y_ref[0, k1], y_ref[1, k1]], axis=0).astype(BF16)
        o_ref[:, k1, :] = _dot(m_ref[k1], x)


def _dense_dft_kernel(m_ref, v_ref, prev_ref, o_ref):
    del prev_ref
    x = jnp.concatenate([v_ref[0], v_ref[1]], axis=0)
    o_ref[...] = _dot(m_ref[...], x).astype(o_ref.dtype)


def _dft_tables(seq, ctx_len):
    n1 = n2 = int(round(math.sqrt(seq)))
    assert n1 * n2 == seq
    a1 = 2.0 * np.pi * np.outer(np.arange(n1), np.arange(n1)) / n1
    c1, s1 = np.cos(a1), np.sin(a1)
    w1 = np.block([[c1, s1], [-s1, c1]])
    k = np.arange(n1)[:, None, None] + n1 * np.arange(n2)[None, :, None]
    a2 = 2.0 * np.pi * ((k * np.arange(n2)[None, None, :]) % seq) / seq
    m2 = np.concatenate([np.cos(a2), np.sin(a2)], axis=2) / math.sqrt(seq)
    ac = 2.0 * np.pi * np.outer(np.arange(ctx_len), np.arange(ctx_len)) / ctx_len
    mc = np.concatenate([np.cos(ac), np.sin(ac)], axis=1) / math.sqrt(ctx_len)
    return n1, n2, jnp.asarray(w1, BF16), jnp.asarray(m2, BF16), jnp.asarray(mc, BF16)


def _fourier_positions(v, tables, *, batch, seq, ctx_len, with_ctx, per=8):
    n1, n2, w1, m2, mc = tables
    _, n_all, cols = v.shape
    v_lat = v.reshape(2, n_all // n2, n2 * cols)
    o = pl.pallas_call(
        functools.partial(_dft_kernel, per=min(per, n2)),
        grid=(batch,),
        in_specs=[pl.BlockSpec((2 * n1, 2 * n1), lambda b: (0, 0)),
                  pl.BlockSpec((n1, n2, 2 * n2), lambda b: (0, 0, 0)),
                  pl.BlockSpec((2, n1, n2 * cols), lambda b: (0, b, 0))],
        out_specs=pl.BlockSpec((n2, n1, cols), lambda b: (b, 0, 0)),
        out_shape=jax.ShapeDtypeStruct((n_all // n1, n1, cols), F32),
        scratch_shapes=[pltpu.VMEM((2, n1, n2, cols), F32)],
        compiler_params=_cparams("parallel"),
        name="dft_latent",
    )(w1, m2, v_lat)
    o = o.reshape(n_all, cols)
    if not with_ctx:
        return o
    blk0 = batch * seq // ctx_len
    return pl.pallas_call(
        _dense_dft_kernel,
        grid=(batch,),
        in_specs=[pl.BlockSpec((ctx_len, 2 * ctx_len), lambda b: (0, 0)),
                  pl.BlockSpec((2, ctx_len, cols), lambda b: (0, blk0 + b, 0)),
                  pl.BlockSpec(memory_space=pl.ANY)],
        out_specs=pl.BlockSpec((ctx_len, cols), lambda b: (blk0 + b, 0)),
        out_shape=jax.ShapeDtypeStruct((n_all, cols), F32),
        input_output_aliases={2: 0},
        compiler_params=_cparams("parallel"),
        name="dft_ctx",
    )(mc, v, o)


def _out_proj_kernel(oa_ref, ob_ref, of_ref, w_ref, *rest, alpha, n_lat_tiles=None, n_chunks=4):
    i = pl.program_id(0)
    if n_lat_tiles is None:
        x_ref, g_ref, lg_ref, lb_ref, sh_ref, sc_ref, xo_ref, ho_ref, y0_ref, y1_ref = rest
        residual = lambda rows: x_ref[rows, :]
    else:
        xl_ref, xc_ref, g_ref, lg_ref, lb_ref, sh_ref, sc_ref, xo_ref, ho_ref, y0_ref, y1_ref = rest
        latent = jnp.maximum(i - 1, 0) < n_lat_tiles
        residual = lambda rows: jnp.where(latent, xl_ref[rows, :], xc_ref[rows, :])

    @pl.when(i == 0)
    def _():
        y1_ref[...] = jnp.zeros_like(y1_ref)

    def step(y_write, y_read):
        tm, d = y_write.shape
        cw, rc = d // n_chunks, tm // n_chunks
        oa, ob, of = oa_ref[...], ob_ref[...], of_ref[...].astype(BF16)
        gate = None
        for c in range(n_chunks):
            ob_c = ob if gate is None else ob + jnp.concatenate([gate] * (ob.shape[1] // HEAD_DIM), axis=1)
            mixed = jnp.concatenate([oa, ob_c, of], axis=1)
            y_write[:, c * cw:(c + 1) * cw] = _dot(mixed, w_ref[:, c * cw:(c + 1) * cw])
            rows = slice(c * rc, (c + 1) * rc)
            xn = _post_norm(residual(rows), y_read[rows, :], g_ref[0], lg_ref[...], lb_ref[...], alpha)
            xo_ref[rows, :] = xn
            dev = xn - jnp.mean(xn, axis=-1, keepdims=True)
            var = jnp.mean(dev * dev, axis=-1, keepdims=True)
            h = dev * lax.rsqrt(var + EPS) * (1.0 + sc_ref[0]) + sh_ref[0]
            ho_ref[rows, :] = h.astype(ho_ref.dtype)
            gate = _exact_zero(jnp.broadcast_to(jnp.sum(var, axis=0, keepdims=True), (1, HEAD_DIM)))

    @pl.when(i % 2 == 0)
    def _():
        step(y0_ref, y1_ref)

    @pl.when(i % 2 == 1)
    def _():
        step(y1_ref, y0_ref)


def _out_proj(oa, ob, of, w_out, x_res, mod3, lg, lb, *, layer, row_fn, n_rows, tm, alpha):
    n_all = oa.shape[0]
    wa, wb, wf = oa.shape[1], ob.shape[1], of.shape[1]
    d = w_out.shape[2]
    assert wa + wb + wf == w_out.shape[1]
    n_tiles = n_rows // tm
    lead = lambda i: (jnp.minimum(i, n_tiles - 1), 0)
    lag = lambda i: (jnp.maximum(i - 1, 0), 0)
    lag_row = lambda i: row_fn(jnp.maximum(i - 1, 0))
    vec = pl.BlockSpec((1, d), lambda i: (0, 0))
    if isinstance(x_res, tuple):
        n_lat_tiles = x_res[0].shape[0] // tm
        x_args = list(x_res)
        x_specs = [pl.BlockSpec((tm, d), lambda i: (jnp.minimum(lag(i)[0], n_lat_tiles - 1), 0)),
                   pl.BlockSpec((tm, d), lambda i: (jnp.maximum(lag(i)[0] - n_lat_tiles, 0), 0))]
    else:
        n_lat_tiles, x_args, x_specs = None, [x_res], [pl.BlockSpec((tm, d), lag)]
    return pl.pallas_call(
        functools.partial(_out_proj_kernel, alpha=alpha, n_lat_tiles=n_lat_tiles),
        grid=(n_tiles + 1,),
        in_specs=[pl.BlockSpec((tm, wa), lead), pl.BlockSpec((tm, wb), lead), pl.BlockSpec((tm, wf), lead),
                  pl.BlockSpec((None,) + w_out.shape[1:], lambda i: (layer, 0, 0))] + x_specs + [
                  _mod_spec(d, 2, lag_row), vec, vec, _mod_spec(d, 3, lag_row), _mod_spec(d, 4, lag_row)],
        out_specs=[pl.BlockSpec((tm, d), lag), pl.BlockSpec((tm, d), lag)],
        out_shape=[jax.ShapeDtypeStruct((n_all, d), F32), jax.ShapeDtypeStruct((n_all, d), BF16)],
        scratch_shapes=[pltpu.VMEM((tm, d), F32), pltpu.VMEM((tm, d), F32)],
        compiler_params=_cparams("arbitrary"),
        name="out_proj",
    )(oa, ob, of, w_out, *x_args, mod3, lg, lb, mod3, mod3)


HALO = 16


def _ffn_kernel(hp_ref, hm_ref, hn_ref, wgu_ref, cwb_ref, wd_ref, x_ref, g_ref, lg_ref,
                lb_ref, *rest, tm, n_tiles, n_lat_tiles, seq, ctx_len, alpha, emit_h, n_split=2, n_epi=4):
    if emit_h:
        sh_ref, sc_ref, xo_ref, ho_ref, hext_ref, acc_ref = rest
    else:
        xo_ref, hext_ref, acc_ref = rest
    i = pl.program_id(0)
    j = pl.program_id(1)
    slot = i % 2
    rc = tm // n_epi

    def epilogue_chunk():
        rows = pl.ds(pl.multiple_of(j * rc, rc), rc)
        v = alpha * x_ref[rows, :] + g_ref[0] * acc_ref[1 - slot, rows, :]
        dev = v - jnp.mean(v, axis=-1, keepdims=True)
        var = jnp.mean(dev * dev, axis=-1, keepdims=True)
        xn = dev * lax.rsqrt(var + EPS) * lg_ref[...] + lb_ref[...]
        xo_ref[rows, :] = xn
        if emit_h:
            dev = xn - jnp.mean(xn, axis=-1, keepdims=True)
            var = jnp.mean(dev * dev, axis=-1, keepdims=True)
            h = dev * lax.rsqrt(var + EPS) * (1.0 + sc_ref[0]) + sh_ref[0]
            ho_ref[rows, :] = h.astype(ho_ref.dtype)
        return _exact_zero(jnp.broadcast_to(jnp.sum(var, axis=0, keepdims=True), (1, HEAD_DIM)))

    def main(tie):
        tf = wd_ref.shape[0]
        th = tf // n_split
        period = jnp.where(i < n_lat_tiles, seq, ctx_len)
        pos = (i * tm + lax.broadcasted_iota(jnp.int32, (tm, th), 0)) & (period - 1)
        first_row, last_row = pos == 0, pos == period - 1
        h_ext, h_main = hext_ref[...], hm_ref[...]
        part = None
        for s in range(n_split):
            sl = slice(s * th, (s + 1) * th)
            gext = _dot(h_ext, wgu_ref[:, sl])
            u = _dot(h_main, wgu_ref[:, tf + s * th:tf + (s + 1) * th])
            g_prev = jnp.where(first_row, 0.0, gext[HALO - 1:HALO - 1 + tm])
            g_cur = gext[HALO:HALO + tm]
            g_next = jnp.where(last_row, 0.0, gext[HALO + 1:HALO + 1 + tm])
            n_taps = cwb_ref.shape[0] - 1
            cw, cb = cwb_ref[0:n_taps, sl], cwb_ref[n_taps:n_taps + 1, sl]
            if tie is not None and s == 0:
                cb = cb + jnp.concatenate([tie.astype(F32)] * (th // HEAD_DIM), axis=1)
            gc = cb + cw[0:1] * g_prev + cw[1:2] * g_cur + cw[2:3] * g_next
            act = (gc * jax.nn.sigmoid(gc) * u).astype(BF16)
            d_part = _dot(act, wd_ref[sl, :])
            part = d_part if part is None else part + d_part
        acc_ref[slot] += part

    active = i < n_tiles

    @pl.when(jnp.logical_and(active, j == 0))
    def _():
        hext_ref[0:HALO] = hp_ref[...]
        hext_ref[HALO:HALO + tm] = hm_ref[...]
        hext_ref[HALO + tm:] = hn_ref[...]
        acc_ref[slot] = jnp.zeros((tm, acc_ref.shape[2]), F32)

    @pl.when(jnp.logical_and(i == 0, j == 0))
    def _():
        acc_ref[1] = jnp.zeros((tm, acc_ref.shape[2]), F32)

    @pl.when(jnp.logical_and(active, j < n_epi))
    def _():
        main(epilogue_chunk())

    @pl.when(jnp.logical_and(active, j >= n_epi))
    def _():
        main(None)

    @pl.when(jnp.logical_and(jnp.logical_not(active), j < n_epi))
    def _():
        epilogue_chunk()


def _ffn_weight_blocks(w_gate, w_up, conv_w, conv_b, tf):
    depth, d, d_ff = w_up.shape
    nj = d_ff // tf
    blocks = lambda w: w.astype(BF16).reshape(depth, d, nj, tf).transpose(0, 2, 1, 3)
    wgu = jnp.concatenate([blocks(w_gate), blocks(w_up)], axis=3)
    cwb = jnp.concatenate([conv_w, conv_b[:, None, :]], axis=1)
    cwb = cwb.reshape(depth, cwb.shape[1], nj, tf).transpose(0, 2, 1, 3)
    return wgu, cwb


def _conv_ffn(h, wgu, cwb, w_down, x_all, mod3, lg, lb, mod3_next, *, layer, row_fn,
              n_rows, n_out_rows, tm, seq, ctx_len, n_lat_tiles, alpha):
    n_all, d = x_all.shape
    nj, tf = wgu.shape[1], wgu.shape[3] // 2
    d_ff = nj * tf
    emit_h = mod3_next is not None
    per = tm // HALO
    last_halo = n_rows // HALO - 1
    n_tiles, nj = n_rows // tm, d_ff // tf
    n_epi = min(4, nj)
    lead = lambda i: jnp.minimum(i, n_tiles - 1)
    lag = lambda i: jnp.maximum(i - 1, 0)
    col = lambda i, j: jnp.where(i < n_tiles, j, nj - 1)
    lag_row = lambda i: row_fn(lag(i))
    vec = pl.BlockSpec((1, d), lambda i, j: (0, 0))
    in_specs = [pl.BlockSpec((HALO, d), lambda i, j: (jnp.maximum(lead(i) * per - 1, 0), 0)),
                pl.BlockSpec((tm, d), lambda i, j: (lead(i), 0)),
                pl.BlockSpec((HALO, d), lambda i, j: (jnp.minimum((lead(i) + 1) * per, last_halo), 0)),
                pl.BlockSpec((None, None, d, 2 * tf), lambda i, j: (layer, col(i, j), 0, 0)),
                pl.BlockSpec((None, None, cwb.shape[2], tf), lambda i, j: (layer, col(i, j), 0, 0)),
                pl.BlockSpec((None, tf, d), lambda i, j: (layer, col(i, j), 0)),
                pl.BlockSpec((tm, d), lambda i, j: (lag(i), 0)),
                _mod_spec(d, 5, lag_row), vec, vec]
    args = [h, h, h, wgu, cwb, w_down, x_all, mod3, lg, lb]
    out_specs = [pl.BlockSpec((tm, d), lambda i, j: (lag(i), 0))]
    out_shape = [jax.ShapeDtypeStruct((n_out_rows, d), F32)]
    if emit_h:
        in_specs += [_mod_spec(d, 0, lag_row), _mod_spec(d, 1, lag_row)]
        args += [mod3_next, mod3_next]
        out_specs.append(pl.BlockSpec((tm, d), lambda i, j: (lag(i), 0)))
        out_shape.append(jax.ShapeDtypeStruct((n_out_rows, d), BF16))
    kern = functools.partial(_ffn_kernel, tm=tm, n_tiles=n_tiles, n_lat_tiles=n_lat_tiles, seq=seq,
                             ctx_len=ctx_len, alpha=alpha, emit_h=emit_h, n_epi=n_epi)
    return pl.pallas_call(
        kern,
        grid=(n_tiles + 1, nj),
        in_specs=in_specs,
        out_specs=out_specs,
        out_shape=out_shape,
        scratch_shapes=[pltpu.VMEM((tm + 2 * HALO, d), BF16), pltpu.VMEM((2, tm, d), F32)],
        compiler_params=_cparams("arbitrary", "arbitrary"),
        name="conv_ffn",
    )(*args)


def _rope_tables(seq, pad_rows):
    rows = seq // GRID_W
    row = jnp.repeat(jnp.arange(rows), GRID_W).astype(F32)
    col = jnp.tile(jnp.arange(GRID_W), rows).astype(F32)
    n_freq = HEAD_DIM // 4
    inv_freq = ROPE_BASE ** (-jnp.arange(n_freq, dtype=F32) / n_freq)
    ar, ac = row[:, None] * inv_freq, col[:, None] * inv_freq
    zero = jnp.zeros_like(ar)
    cos = jnp.concatenate([jnp.cos(ar), jnp.cos(ar), jnp.cos(ac), jnp.cos(ac)], axis=1)
    s1 = jnp.concatenate([-jnp.sin(ar), zero, -jnp.sin(ac), zero], axis=1)
    s2 = jnp.concatenate([zero, jnp.sin(ar), zero, jnp.sin(ac)], axis=1)
    pad = lambda t, v: jnp.concatenate([t, jnp.full((pad_rows, HEAD_DIM), v, F32)], axis=0)
    return pad(cos, 1.0), pad(s1, 0.0), pad(s2, 0.0)


def _forward(x, c, ctx, c_ctx, w_mod, b_mod, w_in, q_gain_a, k_gain_a, sink_b, w_fourier, w_out,
             ln1_g, ln1_b, w_up, w_gate, conv_w, conv_b, w_down, ln2_g, ln2_b,
             *, tm=512, tm_in=512, tm_out=256, tn_in=512, tf=512, tq=512, tk=512):
    batch, seq, d = x.shape
    ctx_len = ctx.shape[1]
    depth = w_mod.shape[0]
    n_lat, n_ctx = batch * seq, batch * ctx_len
    n_all = n_lat + n_ctx
    qa_w, qb_w, f_w = d // 2, d // 4, d // 4
    ka_w, kb_w = A_KV_HEADS * HEAD_DIM, B_KV_HEADS * HEAD_DIM
    assert f_w == F_GROUPS * HEAD_DIM and batch + 1 <= MOD_ROWS
    assert seq & (seq - 1) == 0 and ctx_len & (ctx_len - 1) == 0
    widths = (("qa", qa_w), ("qb", qb_w), ("ka", ka_w), ("v", ka_w), ("kb", kb_w), ("v", kb_w), ("f", f_w))
    kinds = tuple(kind for kind, w in widths for _ in range(w // HEAD_DIM))
    qb_off = qa_w
    ka_off = qa_w + qb_w
    kb_off = ka_off + 2 * ka_w
    f_off = kb_off + 2 * kb_w
    alpha = (2.0 * depth) ** 0.25
    scale = HEAD_DIM ** -0.5 * LOG2_E

    def row_fn_for(t):
        return lambda i: jnp.where(i < n_lat // t, i // (seq // t), batch)

    def rope_fn(i):
        return jnp.where(i < n_lat // tm_in, i % (seq // tm_in), seq // tm_in)

    cc = jnp.concatenate([c, c_ctx[None], jnp.zeros((MOD_ROWS - batch - 1, d), F32)], axis=0)
    mod = _mod_all(cc, w_mod, b_mod)
    wfold = _fourier_fold(w_fourier)
    cos, s1, s2 = _rope_tables(seq, tm_in)
    tables = _dft_tables(seq, ctx_len)
    mod3 = [mod[l].reshape(MOD_ROWS, 1, 6 * d) for l in range(depth)]
    vec = lambda t: t.reshape(1, -1)

    w_in_b, w_out_b = w_in.astype(BF16), w_out.astype(BF16)
    w_down_b = w_down.astype(BF16)
    wgu, cwb = _ffn_weight_blocks(w_gate, w_up, conv_w, conv_b, tf)

    x_cur = (x.reshape(n_lat, d), ctx.reshape(n_ctx, d))
    h = _ln_mod(*x_cur, mod3[0], row_fn_for(tm), tm)
    for l in range(depth):
        last = l == depth - 1
        p, vt3 = _in_proj(h, w_in_b, vec(q_gain_a[l]), vec(k_gain_a[l]), cos, s1, s2, layer=l, kinds=kinds,
                          rope_fn=rope_fn, tm=tm_in, tn=tn_in, scale=scale)
        common = dict(batch=batch, seq=seq, ctx_len=ctx_len, tq=tq, tk=tk)
        ga = dict(window=False, q_off=0, k_off=ka_off, v_row=0, group=qa_w // ka_w, out_width=qa_w, **common)
        gb = dict(window=True, q_off=qb_off, k_off=kb_off, v_row=A_KV_HEADS, group=qb_w // kb_w,
                  out_width=qb_w, **common)
        o_a = _attention(p, vt3, None, None, latent_queries=True, **ga)
        o_b = _attention(p, vt3, None, sink_b[l], latent_queries=True, **gb)
        v = _fourier_channels(p, wfold[l], f_off, 2 * tm if n_all % (2 * tm) == 0 else tm)
        if not last:
            o_a = _attention(p, vt3, o_a, None, latent_queries=False, **ga)
            o_b = _attention(p, vt3, o_b, sink_b[l], latent_queries=False, **gb)
        o_f = _fourier_positions(v, tables, batch=batch, seq=seq, ctx_len=ctx_len, with_ctx=not last)
        n_rows = n_lat if last else n_all
        x_mid, h_mid = _out_proj(o_a, o_b, o_f, w_out_b, x_cur, mod3[l], vec(ln1_g[l]), vec(ln1_b[l]),
                                 layer=l, row_fn=row_fn_for(tm_out), n_rows=n_rows, tm=tm_out, alpha=alpha)
        outs = _conv_ffn(h_mid, wgu, cwb, w_down_b, x_mid,
                         mod3[l], vec(ln2_g[l]), vec(ln2_b[l]), None if last else mod3[l + 1],
                         layer=l, row_fn=row_fn_for(tm), n_rows=n_rows, n_out_rows=n_rows, tm=tm, seq=seq,
                         ctx_len=ctx_len, n_lat_tiles=n_lat // tm, alpha=alpha)
        if last:
            x_cur = outs[0]
        else:
            x_cur, h = outs
    return x_cur.reshape(batch, seq, d)


def kernel(x, c, ctx, c_ctx, w_mod, b_mod, w_in, q_gain_a, k_gain_a, sink_b, w_fourier, w_out, ln1_g, ln1_b,
           w_up, w_gate, conv_w, conv_b, w_down, ln2_g, ln2_b):
    return _forward(x, c, ctx, c_ctx, w_mod, b_mod, w_in, q_gain_a, k_gain_a, sink_b, w_fourier, w_out,
                    ln1_g, ln1_b, w_up, w_gate, conv_w, conv_b, w_down, ln2_g, ln2_b)
```

```python
import functools
import math

import numpy as np
import jax
import jax.numpy as jnp
from jax import lax
from jax.experimental import pallas as pl
from jax.experimental.pallas import tpu as pltpu

F32 = jnp.float32
BF16 = jnp.bfloat16

HEAD_DIM = 128
A_KV_HEADS = 2
B_KV_HEADS = 2
F_GROUPS = 4
WINDOW = 128
GRID_W = 64
ROPE_BASE = 10000.0
EPS = 1e-6
NEG_INF = -1e30
LOG2_E = math.log2(math.e)
MOD_ROWS = 16
VMEM_LIMIT_BYTES = 56 * 1024 * 1024


def _cparams(*sem):
    return pltpu.CompilerParams(dimension_semantics=sem, vmem_limit_bytes=VMEM_LIMIT_BYTES)


def _dot(a, b):
    return jnp.dot(a, b, preferred_element_type=F32)


def _ln(v):
    mu = jnp.mean(v, axis=-1, keepdims=True)
    d = v - mu
    var = jnp.mean(d * d, axis=-1, keepdims=True)
    return d * lax.rsqrt(var + EPS)


def _post_norm(x, y, gate, lg, lb, alpha):
    return _ln(alpha * x + gate * y) * lg + lb


def _modulate(x, sh, sc):
    return _ln(x) * (1.0 + sc) + sh


def _mod_kernel(c_ref, w_ref, b_ref, o_ref):
    a = c_ref[...]
    a = a * jax.nn.sigmoid(a)
    a_hi = a.astype(BF16)
    a_lo = (a - a_hi.astype(F32)).astype(BF16)
    w = w_ref[0]
    w_hi = w.astype(BF16)
    w_lo = (w - w_hi.astype(F32)).astype(BF16)
    both = _dot(jnp.concatenate([a_hi, a_lo], axis=0), w_hi)
    o_ref[0] = both[:MOD_ROWS] + both[MOD_ROWS:] + _dot(a_hi, w_lo) + b_ref[0]


def _mod_all(cc, w_mod, b_mod, tn=1024):
    depth, d, n = w_mod.shape
    return pl.pallas_call(
        _mod_kernel,
        grid=(depth, n // tn),
        in_specs=[pl.BlockSpec((MOD_ROWS, d), lambda l, j: (0, 0)),
                  pl.BlockSpec((1, d, tn), lambda l, j: (l, 0, j)),
                  pl.BlockSpec((1, 1, tn), lambda l, j: (l, 0, j))],
        out_specs=pl.BlockSpec((1, MOD_ROWS, tn), lambda l, j: (l, 0, j)),
        out_shape=jax.ShapeDtypeStruct((depth, MOD_ROWS, n), F32),
        compiler_params=_cparams("parallel", "parallel"),
        name="mod_all",
    )(cc, w_mod, b_mod.reshape(depth, 1, n))


def _fold_kernel(cs_ref, w_ref, o_ref):
    o_ref[0] = jnp.dot(cs_ref[...], w_ref[0], preferred_element_type=F32,
                       precision=lax.Precision.HIGHEST)


def _fourier_fold(w_fourier):
    depth, groups, c, _ = w_fourier.shape
    k = np.arange(c)
    ang = 2.0 * np.pi * np.outer(k, k) / c
    cs = np.concatenate([np.cos(ang), -np.sin(ang)], axis=0) / math.sqrt(c)
    out = pl.pallas_call(
        _fold_kernel,
        grid=(depth * groups,),
        in_specs=[pl.BlockSpec((2 * c, c), lambda i: (0, 0)),
                  pl.BlockSpec((1, c, c), lambda i: (i, 0, 0))],
        out_specs=pl.BlockSpec((1, 2 * c, c), lambda i: (i, 0, 0)),
        out_shape=jax.ShapeDtypeStruct((depth * groups, 2 * c, c), F32),
        compiler_params=_cparams("parallel"),
        name="fourier_fold",
    )(jnp.asarray(cs, F32), w_fourier.reshape(depth * groups, c, c))
    out = out.reshape(depth, groups, 2, c, c).transpose(0, 1, 3, 2, 4).reshape(depth, groups, c, 2 * c)
    return out.astype(BF16)


def _ln_mod_kernel(x_ref, c_ref, sh_ref, sc_ref, h_ref, *, n_lat_tiles):
    latent = pl.program_id(0) < n_lat_tiles

    @pl.when(latent)
    def _():
        h_ref[...] = _modulate(x_ref[...], sh_ref[0], sc_ref[0]).astype(h_ref.dtype)

    @pl.when(jnp.logical_not(latent))
    def _():
        h_ref[...] = _modulate(c_ref[...], sh_ref[0], sc_ref[0]).astype(h_ref.dtype)


def _mod_spec(d, chunk, row_fn):
    return pl.BlockSpec((1, 1, d), lambda i, *_: (row_fn(i), 0, chunk))


def _ln_mod(x_lat, x_ctx, mod3, row_fn, tm):
    n_lat, d = x_lat.shape
    n_ctx = x_ctx.shape[0]
    n_lat_tiles, n = n_lat // tm, n_lat + n_ctx
    return pl.pallas_call(
        functools.partial(_ln_mod_kernel, n_lat_tiles=n_lat_tiles),
        grid=(n // tm,),
        in_specs=[pl.BlockSpec((tm, d), lambda i: (jnp.minimum(i, n_lat_tiles - 1), 0)),
                  pl.BlockSpec((tm, d), lambda i: (jnp.maximum(i - n_lat_tiles, 0), 0)),
                  _mod_spec(d, 0, row_fn), _mod_spec(d, 1, row_fn)],
        out_specs=pl.BlockSpec((tm, d), lambda i: (i, 0)),
        out_shape=jax.ShapeDtypeStruct((n, d), BF16),
        compiler_params=_cparams("parallel"),
        name="ln_mod",
    )(x_lat, x_ctx, mod3, mod3)


def _rope(t, cos, s1, s2):
    return t * cos + pltpu.roll(t, HEAD_DIM - 32, 1) * s1 + pltpu.roll(t, 32, 1) * s2


def _rms(t, gain):
    return t * lax.rsqrt(jnp.mean(t * t, axis=-1, keepdims=True) + EPS) * gain


def _win_kernel(h_ref, w_ref, qg_ref, kg_ref, cos_ref, s1_ref, s2_ref, p_ref, vt_ref, *,
                kinds, tm, tn, scale):
    h = h_ref[...]
    cos, s1, s2 = cos_ref[...], s1_ref[...], s2_ref[...]
    per = tn // HEAD_DIM
    n_v = 0
    for b in range(len(kinds) // per):
        acc = _dot(h, w_ref[:, b * tn:(b + 1) * tn])
        for c in range(per):
            kind = kinds[b * per + c]
            col = (b * per + c) * HEAD_DIM
            t = acc[:, c * HEAD_DIM:(c + 1) * HEAD_DIM]
            if kind == "qa":
                t = _rope(_rms(t, qg_ref[...]), cos, s1, s2) * scale
            elif kind == "qb":
                t = _rope(t, cos, s1, s2) * scale
            elif kind == "ka":
                t = _rope(_rms(t, kg_ref[...]), cos, s1, s2)
            elif kind == "kb":
                t = _rope(t, cos, s1, s2)
            p_ref[:, col:col + HEAD_DIM] = t.astype(p_ref.dtype)
            if kind == "v":
                for r in range(tm // HEAD_DIM):
                    tile = t[r * HEAD_DIM:(r + 1) * HEAD_DIM]
                    vt_ref[r, n_v * HEAD_DIM:(n_v + 1) * HEAD_DIM, :] = tile.T.astype(vt_ref.dtype)
                n_v += 1


def _in_proj(h, w, qg, kg, cos, s1, s2, *, layer, kinds, rope_fn, tm, tn, scale):
    n, d = h.shape
    in_w = w.shape[2]
    n_v = kinds.count("v")
    kern = functools.partial(_win_kernel, kinds=kinds, tm=tm, tn=tn, scale=scale)
    tab = pl.BlockSpec((tm, HEAD_DIM), lambda i: (rope_fn(i), 0))
    vec = pl.BlockSpec((1, HEAD_DIM), lambda i: (0, 0))
    return pl.pallas_call(
        kern,
        grid=(n // tm,),
        in_specs=[pl.BlockSpec((tm, d), lambda i: (i, 0)),
                  pl.BlockSpec((None, d, in_w), lambda i: (layer, 0, 0)),
                  vec, vec, tab, tab, tab],
        out_specs=[pl.BlockSpec((tm, in_w), lambda i: (i, 0)),
                   pl.BlockSpec((tm // HEAD_DIM, n_v * HEAD_DIM, HEAD_DIM), lambda i: (i, 0, 0))],
        out_shape=[jax.ShapeDtypeStruct((n, in_w), BF16),
                   jax.ShapeDtypeStruct((n // HEAD_DIM, n_v * HEAD_DIM, HEAD_DIM), BF16)],
        compiler_params=_cparams("parallel"),
        name="in_proj",
    )(h, w, qg, kg, cos, s1, s2)


def _scores_t(k, q):
    return lax.dot_general(k, q, (((1,), (1,)), ((), ())), preferred_element_type=F32)


def _vt_tile(vt_ref, first, count):
    tiles = [vt_ref[first + t] for t in range(count)]
    return tiles[0] if count == 1 else jnp.concatenate(tiles, axis=1)


def _online_first(s, vt):
    m = jnp.max(s, axis=0, keepdims=True)
    p = jnp.exp2(s - m)
    return m, jnp.sum(p, axis=0, keepdims=True), _dot(vt, p.astype(BF16))


def _exact_zero(acc):
    bits = pltpu.bitcast(acc[0:1, 0:HEAD_DIM], jnp.uint32)
    bits = lax.shift_right_logical(lax.shift_right_logical(bits, jnp.uint32(16)), jnp.uint32(16))
    return pltpu.bitcast(bits, F32).astype(BF16)


SUM_ROWS = 16


def _with_ones(vt):
    return jnp.concatenate([vt, jnp.ones((SUM_ROWS, vt.shape[1]), vt.dtype)], axis=0)


def _online_step(s, vt_ones, m, acc):
    m_new = jnp.maximum(m, jnp.max(s, axis=0, keepdims=True))
    p = jnp.exp2(s - m_new).astype(BF16)
    return m_new, jnp.exp2(m - m_new) * acc + _dot(vt_ones, p)


def _global_attn_kernel(*refs, group, tk, n_chunks, ctx_len):
    if n_chunks:
        q_ref, k_ref, vt_ref, kc_ref, vtc_ref, o_ref = refs[:6]
        s_refs = refs[6:6 + 2 * group]
        acc_refs = refs[6 + 2 * group:]
    else:
        q_ref, kc_ref, vtc_ref, o_ref = refs
    sub = tk // HEAD_DIM
    tq = q_ref.shape[0]
    cols = [slice(g * HEAD_DIM, (g + 1) * HEAD_DIM) for g in range(group)]
    qs = [q_ref[:, c] for c in cols]
    kc, vtc = kc_ref[...], _vt_tile(vtc_ref, 0, ctx_len // HEAD_DIM)
    if not n_chunks:
        for c, q in zip(cols, qs):
            _, l, acc = _online_first(_scores_t(kc, q), vtc)
            o_ref[:, c] = (acc / l).T.astype(o_ref.dtype)
    else:
        assert n_chunks % 2 == 0

        def lat_keys(c):
            return k_ref[pl.ds(pl.multiple_of(c * tk, tk), tk), :]

        def lat_values(c):
            return _with_ones(_vt_tile(vt_ref, c * sub, sub))

        def stage(k_next, slot_next, vt_cur, slot_cur, ms):
            out, gate = [], None
            for g, q in enumerate(qs):
                if k_next is not None:
                    n_keys = k_next.shape[0]
                    s_refs[slot_next * group + g][0:n_keys, :] = _scores_t(
                        k_next if gate is None else k_next + gate, q)
                if vt_cur is not None:
                    n_keys = vt_cur.shape[1]
                    m_new, acc = _online_step(s_refs[slot_cur * group + g][0:n_keys, :], vt_cur, ms[g],
                                              acc_refs[g][...])
                    acc_refs[g][...] = acc
                    gate = _exact_zero(m_new)
                    out.append(m_new)
            return tuple(out)

        def body(i, ms):
            ms = stage(lat_keys(2 * i + 1), 1, lat_values(2 * i), 0, ms)
            return stage(lat_keys(2 * i + 2), 0, lat_values(2 * i + 1), 1, ms)

        for acc_ref in acc_refs:
            acc_ref[...] = jnp.zeros_like(acc_ref)
        ms = (jnp.full((1, tq), -jnp.inf, F32),) * group
        stage(lat_keys(0), 0, None, None, None)
        ms = lax.fori_loop(0, n_chunks // 2 - 1, body, ms)
        ms = stage(lat_keys(n_chunks - 1), 1, lat_values(n_chunks - 2), 0, ms)
        ms = stage(kc, 0, lat_values(n_chunks - 1), 1, ms)
        stage(None, None, _with_ones(vtc), 0, ms)
        for c, acc_ref in zip(cols, acc_refs):
            o = acc_ref[0:HEAD_DIM, :] / acc_ref[HEAD_DIM:HEAD_DIM + 1, :]
            o_ref[:, c] = o.T.astype(o_ref.dtype)


def _window_attn_kernel(*refs, group, tq, seq, ctx_len, local):
    if local:
        sink_ref, bias_ref, q_ref, k_ref, vt_ref, kc_ref, vtc_ref, o_ref = refs
    else:
        sink_ref, q_ref, kc_ref, vtc_ref, o_ref = refs
    kv = pl.program_id(1)
    qi = pl.program_id(2)
    sub = tq // HEAD_DIM
    n_sub = seq // HEAD_DIM
    kc, vtc = kc_ref[...], _vt_tile(vtc_ref, 0, ctx_len // HEAD_DIM)
    if local:
        first = qi * sub
        prev = jnp.maximum(first - 1, 0)
        nxt = jnp.minimum(first + sub, n_sub - 1)
        k_loc = jnp.concatenate(
            [k_ref[pl.ds(pl.multiple_of(prev * HEAD_DIM, HEAD_DIM), HEAD_DIM), :],
             k_ref[pl.ds(pl.multiple_of(first * HEAD_DIM, tq), tq), :],
             k_ref[pl.ds(pl.multiple_of(nxt * HEAD_DIM, HEAD_DIM), HEAD_DIM), :]], axis=0)
        vt_loc = jnp.concatenate(
            [vt_ref[prev]] + [vt_ref[first + t] for t in range(sub)] + [vt_ref[nxt]], axis=1)
        bias = bias_ref[...]

    for g in range(group):
        cols = slice(g * HEAD_DIM, (g + 1) * HEAD_DIM)
        q = q_ref[:, cols]
        sink = sink_ref[kv * group + g] * LOG2_E
        s_c = _scores_t(kc, q)
        m = jnp.maximum(jnp.max(s_c, axis=0, keepdims=True), sink)
        if local:
            s_l = _scores_t(k_loc, q) + bias
            m = jnp.maximum(m, jnp.max(s_l, axis=0, keepdims=True))
        p_c = jnp.exp2(s_c - m)
        denom = jnp.sum(p_c, axis=0, keepdims=True) + jnp.exp2(sink - m)
        acc = _dot(vtc, p_c.astype(BF16))
        if local:
            p_l = jnp.exp2(s_l - m)
            denom = denom + jnp.sum(p_l, axis=0, keepdims=True)
            acc = acc + _dot(vt_loc, p_l.astype(BF16))
        o_ref[:, cols] = (acc / denom).T.astype(o_ref.dtype)


def _window_bias(n_loc, tq):
    key = jnp.arange(n_loc)[:, None] - HEAD_DIM
    band = jnp.abs(key - jnp.arange(tq)[None, :]) <= WINDOW
    variants = [band & (key >= 0), band, band & (key < tq)]
    return jnp.stack([jnp.where(v, 0.0, NEG_INF).astype(F32) for v in variants])


def _attention(p, vt3, out_prev, sink, *, window, latent_queries, batch, seq, ctx_len, q_off, k_off,
               v_row, group, out_width, tq, tk):
    n_all = p.shape[0]
    gw = group * HEAD_DIM
    ctx_blk0 = batch * seq // ctx_len
    kc_spec = pl.BlockSpec((ctx_len, HEAD_DIM), lambda b, kv, qi, *_: (ctx_blk0 + b, k_off // HEAD_DIM + kv))
    vtc_spec = pl.BlockSpec((ctx_len // HEAD_DIM, HEAD_DIM, HEAD_DIM),
                            lambda b, kv, qi, *_: (ctx_blk0 + b, v_row + kv, 0))
    if latent_queries:
        n_q = seq // tq
        q_blk = lambda b, kv, qi, *_: (b * n_q + qi, q_off // gw + kv)
        lat_specs = [pl.BlockSpec((seq, HEAD_DIM), lambda b, kv, qi, *_: (b, k_off // HEAD_DIM + kv)),
                     pl.BlockSpec((seq // HEAD_DIM, HEAD_DIM, HEAD_DIM), lambda b, kv, qi, *_: (b, v_row + kv, 0))]
        lat_args = [p, vt3]
    else:
        tq = ctx_len
        n_q = 1
        q_blk = lambda b, kv, qi, *_: (ctx_blk0 + b, q_off // gw + kv)
        lat_specs, lat_args = [], []
    q_spec = pl.BlockSpec((tq, gw), q_blk)
    o_spec = pl.BlockSpec((tq, gw), lambda b, kv, qi, *_: (q_blk(b, kv, qi)[0], kv))
    if window:
        kern = functools.partial(_window_attn_kernel, group=group, tq=tq, seq=seq, ctx_len=ctx_len,
                                 local=latent_queries)
        pre_specs = [pl.BlockSpec(memory_space=pltpu.SMEM)]
        pre_args = [sink]
        if latent_queries:
            assert n_q >= 2
            n_loc = tq + 2 * HEAD_DIM
            pre_specs.append(pl.BlockSpec(
                (None, n_loc, tq), lambda b, kv, qi: (jnp.where(qi == 0, 0, jnp.where(qi == n_q - 1, 2, 1)), 0, 0)))
            pre_args.append(_window_bias(n_loc, tq))
    else:
        kern = functools.partial(_global_attn_kernel, group=group, tk=tk,
                                 n_chunks=seq // tk if latent_queries else 0, ctx_len=ctx_len)
        pre_specs, pre_args = [], []
    scratch = []
    if latent_queries and not window:
        assert ctx_len <= tk
        scratch = [pltpu.VMEM((tk, tq), F32)] * (2 * group) + [pltpu.VMEM((HEAD_DIM + SUM_ROWS, tq), F32)] * group
    in_specs = pre_specs + [q_spec] + lat_specs + [kc_spec, vtc_spec]
    args = pre_args + [p] + lat_args + [p, vt3]
    aliases = {}
    if out_prev is not None:
        in_specs.append(pl.BlockSpec(memory_space=pl.ANY))
        args.append(out_prev)
        aliases = {len(args) - 1: 0}
        body = kern

        def kern(*refs):
            return body(*refs[:-2], refs[-1])
    return pl.pallas_call(
        kern,
        grid=(batch, A_KV_HEADS, n_q),
        in_specs=in_specs,
        out_specs=o_spec,
        out_shape=jax.ShapeDtypeStruct((n_all, out_width), BF16),
        input_output_aliases=aliases,
        scratch_shapes=scratch,
        compiler_params=_cparams("parallel", "parallel", "arbitrary"),
        name=("window" if window else "global") + ("_lat" if latent_queries else "_ctx"),
    )(*args)


def _chan_kernel(u_ref, w_ref, v_ref):
    groups, c, _ = w_ref.shape
    for g in range(groups):
        cols = slice(g * c, (g + 1) * c)
        r = _dot(u_ref[:, cols], w_ref[g])
        v_ref[0, :, cols] = r[:, :c].astype(v_ref.dtype)
        v_ref[1, :, cols] = r[:, c:].astype(v_ref.dtype)


def _fourier_channels(p, wfold_l, f_off, tm):
    n = p.shape[0]
    groups, c, _ = wfold_l.shape
    f_w = groups * c
    assert f_off % f_w == 0
    return pl.pallas_call(
        _chan_kernel,
        grid=(n // tm,),
        in_specs=[pl.BlockSpec((tm, f_w), lambda i: (i, f_off // f_w)),
                  pl.BlockSpec((groups, c, 2 * c), lambda i: (0, 0, 0))],
        out_specs=pl.BlockSpec((2, tm, f_w), lambda i: (0, i, 0)),
        out_shape=jax.ShapeDtypeStruct((2, n, f_w), BF16),
        compiler_params=_cparams("parallel"),
        name="fourier_channels",
    )(p, wfold_l)


def _dft_kernel(w_ref, m_ref, v_ref, o_ref, y_ref, *, per):
    _, n1, n2, cols = y_ref.shape
    tn = per * cols
    for j in range(n2 // per):
        lanes = slice(j * tn, (j + 1) * tn)
        x = jnp.concatenate([v_ref[0, :, lanes], v_ref[1, :, lanes]], axis=0)
        y = _dot(w_ref[...], x)
        for part in range(2):
            for e in range(per):
                y_ref[part, :, j * per + e, :] = y[part * n1:(part + 1) * n1, e * cols:(e + 1) * cols]
    for k1 in range(n1):
        x = jnp.concatenate([y_ref[0, k1], y_ref[1, k1]], axis=0).astype(BF16)
        o_ref[:, k1, :] = _dot(m_ref[k1], x)


def _dense_dft_kernel(m_ref, v_ref, prev_ref, o_ref):
    del prev_ref
    x = jnp.concatenate([v_ref[0], v_ref[1]], axis=0)
    o_ref[...] = _dot(m_ref[...], x).astype(o_ref.dtype)


def _dft_tables(seq, ctx_len):
    n1 = n2 = int(round(math.sqrt(seq)))
    assert n1 * n2 == seq
    a1 = 2.0 * np.pi * np.outer(np.arange(n1), np.arange(n1)) / n1
    c1, s1 = np.cos(a1), np.sin(a1)
    w1 = np.block([[c1, s1], [-s1, c1]])
    k = np.arange(n1)[:, None, None] + n1 * np.arange(n2)[None, :, None]
    a2 = 2.0 * np.pi * ((k * np.arange(n2)[None, None, :]) % seq) / seq
    m2 = np.concatenate([np.cos(a2), np.sin(a2)], axis=2) / math.sqrt(seq)
    ac = 2.0 * np.pi * np.outer(np.arange(ctx_len), np.arange(ctx_len)) / ctx_len
    mc = np.concatenate([np.cos(ac), np.sin(ac)], axis=1) / math.sqrt(ctx_len)
    return n1, n2, jnp.asarray(w1, BF16), jnp.asarray(m2, BF16), jnp.asarray(mc, BF16)


def _fourier_positions(v, tables, *, batch, seq, ctx_len, with_ctx, per=8):
    n1, n2, w1, m2, mc = tables
    _, n_all, cols = v.shape
    v_lat = v.reshape(2, n_all // n2, n2 * cols)
    o = pl.pallas_call(
        functools.partial(_dft_kernel, per=min(per, n2)),
        grid=(batch,),
        in_specs=[pl.BlockSpec((2 * n1, 2 * n1), lambda b: (0, 0)),
                  pl.BlockSpec((n1, n2, 2 * n2), lambda b: (0, 0, 0)),
                  pl.BlockSpec((2, n1, n2 * cols), lambda b: (0, b, 0))],
        out_specs=pl.BlockSpec((n2, n1, cols), lambda b: (b, 0, 0)),
        out_shape=jax.ShapeDtypeStruct((n_all // n1, n1, cols), F32),
        scratch_shapes=[pltpu.VMEM((2, n1, n2, cols), F32)],
        compiler_params=_cparams("parallel"),
        name="dft_latent",
    )(w1, m2, v_lat)
    o = o.reshape(n_all, cols)
    if not with_ctx:
        return o
    blk0 = batch * seq // ctx_len
    return pl.pallas_call(
        _dense_dft_kernel,
        grid=(batch,),
        in_specs=[pl.BlockSpec((ctx_len, 2 * ctx_len), lambda b: (0, 0)),
                  pl.BlockSpec((2, ctx_len, cols), lambda b: (0, blk0 + b, 0)),
                  pl.BlockSpec(memory_space=pl.ANY)],
        out_specs=pl.BlockSpec((ctx_len, cols), lambda b: (blk0 + b, 0)),
        out_shape=jax.ShapeDtypeStruct((n_all, cols), F32),
        input_output_aliases={2: 0},
        compiler_params=_cparams("parallel"),
        name="dft_ctx",
    )(mc, v, o)


def _out_proj_kernel(oa_ref, ob_ref, of_ref, w_ref, *rest, alpha, n_lat_tiles=None, n_chunks=4):
    i = pl.program_id(0)
    if n_lat_tiles is None:
        x_ref, g_ref, lg_ref, lb_ref, sh_ref, sc_ref, xo_ref, ho_ref, y0_ref, y1_ref = rest
        residual = lambda rows: x_ref[rows, :]
    else:
        xl_ref, xc_ref, g_ref, lg_ref, lb_ref, sh_ref, sc_ref, xo_ref, ho_ref, y0_ref, y1_ref = rest
        latent = jnp.maximum(i - 1, 0) < n_lat_tiles
        residual = lambda rows: jnp.where(latent, xl_ref[rows, :], xc_ref[rows, :])

    @pl.when(i == 0)
    def _():
        y1_ref[...] = jnp.zeros_like(y1_ref)

    def step(y_write, y_read):
        tm, d = y_write.shape
        cw, rc = d // n_chunks, tm // n_chunks
        oa, ob, of = oa_ref[...], ob_ref[...], of_ref[...].astype(BF16)
        gate = None
        for c in range(n_chunks):
            ob_c = ob if gate is None else ob + jnp.concatenate([gate] * (ob.shape[1] // HEAD_DIM), axis=1)
            mixed = jnp.concatenate([oa, ob_c, of], axis=1)
            y_write[:, c * cw:(c + 1) * cw] = _dot(mixed, w_ref[:, c * cw:(c + 1) * cw])
            rows = slice(c * rc, (c + 1) * rc)
            xn = _post_norm(residual(rows), y_read[rows, :], g_ref[0], lg_ref[...], lb_ref[...], alpha)
            xo_ref[rows, :] = xn
            dev = xn - jnp.mean(xn, axis=-1, keepdims=True)
            var = jnp.mean(dev * dev, axis=-1, keepdims=True)
            h = dev * lax.rsqrt(var + EPS) * (1.0 + sc_ref[0]) + sh_ref[0]
            ho_ref[rows, :] = h.astype(ho_ref.dtype)
            gate = _exact_zero(jnp.broadcast_to(jnp.sum(var, axis=0, keepdims=True), (1, HEAD_DIM)))

    @pl.when(i % 2 == 0)
    def _():
        step(y0_ref, y1_ref)

    @pl.when(i % 2 == 1)
    def _():
        step(y1_ref, y0_ref)


def _out_proj(oa, ob, of, w_out, x_res, mod3, lg, lb, *, layer, row_fn, n_rows, tm, alpha):
    n_all = oa.shape[0]
    wa, wb, wf = oa.shape[1], ob.shape[1], of.shape[1]
    d = w_out.shape[2]
    assert wa + wb + wf == w_out.shape[1]
    n_tiles = n_rows // tm
    lead = lambda i: (jnp.minimum(i, n_tiles - 1), 0)
    lag = lambda i: (jnp.maximum(i - 1, 0), 0)
    lag_row = lambda i: row_fn(jnp.maximum(i - 1, 0))
    vec = pl.BlockSpec((1, d), lambda i: (0, 0))
    if isinstance(x_res, tuple):
        n_lat_tiles = x_res[0].shape[0] // tm
        x_args = list(x_res)
        x_specs = [pl.BlockSpec((tm, d), lambda i: (jnp.minimum(lag(i)[0], n_lat_tiles - 1), 0)),
                   pl.BlockSpec((tm, d), lambda i: (jnp.maximum(lag(i)[0] - n_lat_tiles, 0), 0))]
    else:
        n_lat_tiles, x_args, x_specs = None, [x_res], [pl.BlockSpec((tm, d), lag)]
    return pl.pallas_call(
        functools.partial(_out_proj_kernel, alpha=alpha, n_lat_tiles=n_lat_tiles),
        grid=(n_tiles + 1,),
        in_specs=[pl.BlockSpec((tm, wa), lead), pl.BlockSpec((tm, wb), lead), pl.BlockSpec((tm, wf), lead),
                  pl.BlockSpec((None,) + w_out.shape[1:], lambda i: (layer, 0, 0))] + x_specs + [
                  _mod_spec(d, 2, lag_row), vec, vec, _mod_spec(d, 3, lag_row), _mod_spec(d, 4, lag_row)],
        out_specs=[pl.BlockSpec((tm, d), lag), pl.BlockSpec((tm, d), lag)],
        out_shape=[jax.ShapeDtypeStruct((n_all, d), F32), jax.ShapeDtypeStruct((n_all, d), BF16)],
        scratch_shapes=[pltpu.VMEM((tm, d), F32), pltpu.VMEM((tm, d), F32)],
        compiler_params=_cparams("arbitrary"),
        name="out_proj",
    )(oa, ob, of, w_out, *x_args, mod3, lg, lb, mod3, mod3)


HALO = 16
FFN_EPILOGUE_STEPS = 8


def _ffn_kernel(hp_ref, hm_ref, hn_ref, wu_ref, wg_ref, cw_ref, cb_ref, wd_ref, x_ref, g_ref, lg_ref,
                lb_ref, *rest, tm, n_tiles, n_lat_tiles, seq, ctx_len, alpha, emit_h, n_epi):
    if emit_h:
        sh_ref, sc_ref, xo_ref, ho_ref, hext_ref, acc_ref = rest
    else:
        xo_ref, hext_ref, acc_ref = rest
    i = pl.program_id(0)
    j = pl.program_id(1)
    slot = i % 2
    rc = tm // n_epi

    def epilogue_chunk():
        rows = pl.ds(pl.multiple_of(j * rc, rc), rc)
        v = alpha * x_ref[rows, :] + g_ref[0] * acc_ref[1 - slot, rows, :]
        dev = v - jnp.mean(v, axis=-1, keepdims=True)
        var = jnp.mean(dev * dev, axis=-1, keepdims=True)
        xn = dev * lax.rsqrt(var + EPS) * lg_ref[...] + lb_ref[...]
        xo_ref[rows, :] = xn
        if emit_h:
            dev = xn - jnp.mean(xn, axis=-1, keepdims=True)
            var = jnp.mean(dev * dev, axis=-1, keepdims=True)
            h = dev * lax.rsqrt(var + EPS) * (1.0 + sc_ref[0]) + sh_ref[0]
            ho_ref[rows, :] = h.astype(ho_ref.dtype)
        return _exact_zero(jnp.broadcast_to(jnp.sum(var, axis=0, keepdims=True), (1, HEAD_DIM)))

    def main(tie):
        tf = wu_ref.shape[1]
        period = jnp.where(i < n_lat_tiles, seq, ctx_len)
        pos = (i * tm + lax.broadcasted_iota(jnp.int32, (tm, tf), 0)) & (period - 1)
        gext = _dot(hext_ref[...], wg_ref[...])
        u = _dot(hm_ref[...], wu_ref[...])
        g_prev = jnp.where(pos == 0, 0.0, gext[HALO - 1:HALO - 1 + tm])
        g_cur = gext[HALO:HALO + tm]
        g_next = jnp.where(pos == period - 1, 0.0, gext[HALO + 1:HALO + 1 + tm])
        cw, cb = cw_ref[...], cb_ref[...]
        if tie is not None:
            cb = cb + jnp.concatenate([tie.astype(F32)] * (tf // HEAD_DIM), axis=1)
        gc = cb + cw[0:1] * g_prev + cw[1:2] * g_cur + cw[2:3] * g_next
        act = (gc * jax.nn.sigmoid(gc) * u).astype(BF16)
        acc_ref[slot] += _dot(act, wd_ref[...])

    active = i < n_tiles

    @pl.when(jnp.logical_and(active, j == 0))
    def _():
        hext_ref[0:HALO] = hp_ref[...]
        hext_ref[HALO:HALO + tm] = hm_ref[...]
        hext_ref[HALO + tm:] = hn_ref[...]
        acc_ref[slot] = jnp.zeros((tm, acc_ref.shape[2]), F32)

    @pl.when(jnp.logical_and(i == 0, j == 0))
    def _():
        acc_ref[1] = jnp.zeros((tm, acc_ref.shape[2]), F32)

    @pl.when(jnp.logical_and(active, j < n_epi))
    def _():
        main(epilogue_chunk())

    @pl.when(jnp.logical_and(active, j >= n_epi))
    def _():
        main(None)

    @pl.when(jnp.logical_and(jnp.logical_not(active), j < n_epi))
    def _():
        epilogue_chunk()


def _conv_ffn(h, w_up, w_gate, conv_w, conv_b, w_down, x_all, mod3, lg, lb, mod3_next, *, layer, row_fn,
              n_rows, n_out_rows, tm, tf, seq, ctx_len, n_lat_tiles, alpha):
    n_all, d = x_all.shape
    d_ff = w_up.shape[2]
    emit_h = mod3_next is not None
    per = tm // HALO
    last_halo = n_rows // HALO - 1
    n_tiles, nj = n_rows // tm, d_ff // tf
    n_epi = min(FFN_EPILOGUE_STEPS, nj)
    lead = lambda i: jnp.minimum(i, n_tiles - 1)
    lag = lambda i: jnp.maximum(i - 1, 0)
    col = lambda i, j: jnp.where(i < n_tiles, j, nj - 1)
    lag_row = lambda i: row_fn(lag(i))
    vec = pl.BlockSpec((1, d), lambda i, j: (0, 0))
    in_specs = [pl.BlockSpec((HALO, d), lambda i, j: (jnp.maximum(lead(i) * per - 1, 0), 0)),
                pl.BlockSpec((tm, d), lambda i, j: (lead(i), 0)),
                pl.BlockSpec((HALO, d), lambda i, j: (jnp.minimum((lead(i) + 1) * per, last_halo), 0)),
                pl.BlockSpec((None, d, tf), lambda i, j: (layer, 0, col(i, j))),
                pl.BlockSpec((None, d, tf), lambda i, j: (layer, 0, col(i, j))),
                pl.BlockSpec((conv_w.shape[0], tf), lambda i, j: (0, col(i, j))),
                pl.BlockSpec((1, tf), lambda i, j: (0, col(i, j))),
                pl.BlockSpec((None, tf, d), lambda i, j: (layer, col(i, j), 0)),
                pl.BlockSpec((tm, d), lambda i, j: (lag(i), 0)),
                _mod_spec(d, 5, lag_row), vec, vec]
    args = [h, h, h, w_up, w_gate, conv_w, conv_b, w_down, x_all, mod3, lg, lb]
    out_specs = [pl.BlockSpec((tm, d), lambda i, j: (lag(i), 0))]
    out_shape = [jax.ShapeDtypeStruct((n_out_rows, d), F32)]
    if emit_h:
        in_specs += [_mod_spec(d, 0, lag_row), _mod_spec(d, 1, lag_row)]
        args += [mod3_next, mod3_next]
        out_specs.append(pl.BlockSpec((tm, d), lambda i, j: (lag(i), 0)))
        out_shape.append(jax.ShapeDtypeStruct((n_out_rows, d), BF16))
    kern = functools.partial(_ffn_kernel, tm=tm, n_tiles=n_tiles, n_lat_tiles=n_lat_tiles, seq=seq,
                             ctx_len=ctx_len, alpha=alpha, emit_h=emit_h, n_epi=n_epi)
    return pl.pallas_call(
        kern,
        grid=(n_tiles + 1, nj),
        in_specs=in_specs,
        out_specs=out_specs,
        out_shape=out_shape,
        scratch_shapes=[pltpu.VMEM((tm + 2 * HALO, d), BF16), pltpu.VMEM((2, tm, d), F32)],
        compiler_params=_cparams("arbitrary", "arbitrary"),
        name="conv_ffn",
    )(*args)


def _rope_tables(seq, pad_rows):
    rows = seq // GRID_W
    row = jnp.repeat(jnp.arange(rows), GRID_W).astype(F32)
    col = jnp.tile(jnp.arange(GRID_W), rows).astype(F32)
    n_freq = HEAD_DIM // 4
    inv_freq = ROPE_BASE ** (-jnp.arange(n_freq, dtype=F32) / n_freq)
    ar, ac = row[:, None] * inv_freq, col[:, None] * inv_freq
    zero = jnp.zeros_like(ar)
    cos = jnp.concatenate([jnp.cos(ar), jnp.cos(ar), jnp.cos(ac), jnp.cos(ac)], axis=1)
    s1 = jnp.concatenate([-jnp.sin(ar), zero, -jnp.sin(ac), zero], axis=1)
    s2 = jnp.concatenate([zero, jnp.sin(ar), zero, jnp.sin(ac)], axis=1)
    pad = lambda t, v: jnp.concatenate([t, jnp.full((pad_rows, HEAD_DIM), v, F32)], axis=0)
    return pad(cos, 1.0), pad(s1, 0.0), pad(s2, 0.0)


def _forward(x, c, ctx, c_ctx, w_mod, b_mod, w_in, q_gain_a, k_gain_a, sink_b, w_fourier, w_out,
             ln1_g, ln1_b, w_up, w_gate, conv_w, conv_b, w_down, ln2_g, ln2_b,
             *, tm=512, tm_in=512, tm_out=256, tn_in=512, tf=512, tq=512, tk=512):
    batch, seq, d = x.shape
    ctx_len = ctx.shape[1]
    depth = w_mod.shape[0]
    n_lat, n_ctx = batch * seq, batch * ctx_len
    n_all = n_lat + n_ctx
    qa_w, qb_w, f_w = d // 2, d // 4, d // 4
    ka_w, kb_w = A_KV_HEADS * HEAD_DIM, B_KV_HEADS * HEAD_DIM
    assert f_w == F_GROUPS * HEAD_DIM and batch + 1 <= MOD_ROWS
    assert seq & (seq - 1) == 0 and ctx_len & (ctx_len - 1) == 0
    widths = (("qa", qa_w), ("qb", qb_w), ("ka", ka_w), ("v", ka_w), ("kb", kb_w), ("v", kb_w), ("f", f_w))
    kinds = tuple(kind for kind, w in widths for _ in range(w // HEAD_DIM))
    qb_off = qa_w
    ka_off = qa_w + qb_w
    kb_off = ka_off + 2 * ka_w
    f_off = kb_off + 2 * kb_w
    alpha = (2.0 * depth) ** 0.25
    scale = HEAD_DIM ** -0.5 * LOG2_E

    def row_fn_for(t):
        return lambda i: jnp.where(i < n_lat // t, i // (seq // t), batch)

    def rope_fn(i):
        return jnp.where(i < n_lat // tm_in, i % (seq // tm_in), seq // tm_in)

    cc = jnp.concatenate([c, c_ctx[None], jnp.zeros((MOD_ROWS - batch - 1, d), F32)], axis=0)
    mod = _mod_all(cc, w_mod, b_mod)
    wfold = _fourier_fold(w_fourier)
    cos, s1, s2 = _rope_tables(seq, tm_in)
    tables = _dft_tables(seq, ctx_len)
    mod3 = [mod[l].reshape(MOD_ROWS, 1, 6 * d) for l in range(depth)]
    vec = lambda t: t.reshape(1, -1)

    w_in_b, w_out_b = w_in.astype(BF16), w_out.astype(BF16)
    w_up_b, w_gate_b, w_down_b = w_up.astype(BF16), w_gate.astype(BF16), w_down.astype(BF16)

    x_cur = (x.reshape(n_lat, d), ctx.reshape(n_ctx, d))
    h = _ln_mod(*x_cur, mod3[0], row_fn_for(tm), tm)
    for l in range(depth):
        last = l == depth - 1
        p, vt3 = _in_proj(h, w_in_b, vec(q_gain_a[l]), vec(k_gain_a[l]), cos, s1, s2, layer=l, kinds=kinds,
                          rope_fn=rope_fn, tm=tm_in, tn=tn_in, scale=scale)
        common = dict(batch=batch, seq=seq, ctx_len=ctx_len, tq=tq, tk=tk)
        ga = dict(window=False, q_off=0, k_off=ka_off, v_row=0, group=qa_w // ka_w, out_width=qa_w, **common)
        gb = dict(window=True, q_off=qb_off, k_off=kb_off, v_row=A_KV_HEADS, group=qb_w // kb_w,
                  out_width=qb_w, **common)
        o_a = _attention(p, vt3, None, None, latent_queries=True, **ga)
        o_b = _attention(p, vt3, None, sink_b[l], latent_queries=True, **gb)
        v = _fourier_channels(p, wfold[l], f_off, 2 * tm if n_all % (2 * tm) == 0 else tm)
        if not last:
            o_a = _attention(p, vt3, o_a, None, latent_queries=False, **ga)
            o_b = _attention(p, vt3, o_b, sink_b[l], latent_queries=False, **gb)
        o_f = _fourier_positions(v, tables, batch=batch, seq=seq, ctx_len=ctx_len, with_ctx=not last)
        n_rows = n_lat if last else n_all
        x_mid, h_mid = _out_proj(o_a, o_b, o_f, w_out_b, x_cur, mod3[l], vec(ln1_g[l]), vec(ln1_b[l]),
                                 layer=l, row_fn=row_fn_for(tm_out), n_rows=n_rows, tm=tm_out, alpha=alpha)
        outs = _conv_ffn(h_mid, w_up_b, w_gate_b, conv_w[l], vec(conv_b[l]), w_down_b, x_mid,
                         mod3[l], vec(ln2_g[l]), vec(ln2_b[l]), None if last else mod3[l + 1],
                         layer=l, row_fn=row_fn_for(tm), n_rows=n_rows, n_out_rows=n_rows, tm=tm, tf=tf, seq=seq,
                         ctx_len=ctx_len, n_lat_tiles=n_lat // tm, alpha=alpha)
        if last:
            x_cur = outs[0]
        else:
            x_cur, h = outs
    return x_cur.reshape(batch, seq, d)


def kernel(x, c, ctx, c_ctx, w_mod, b_mod, w_in, q_gain_a, k_gain_a, sink_b, w_fourier, w_out, ln1_g, ln1_b,
           w_up, w_gate, conv_w, conv_b, w_down, ln2_g, ln2_b):
    return _forward(x, c, ctx, c_ctx, w_mod, b_mod, w_in, q_gain_a, k_gain_a, sink_b, w_fourier, w_out,
                    ln1_g, ln1_b, w_up, w_gate, conv_w, conv_b, w_down, ln2_g, ln2_b)
```

```python
import functools
import math

import numpy as np
import jax
import jax.numpy as jnp
from jax import lax
from jax.experimental import pallas as pl
from jax.experimental.pallas import tpu as pltpu

F32 = jnp.float32
BF16 = jnp.bfloat16

HEAD_DIM = 128
A_KV_HEADS = 2
B_KV_HEADS = 2
F_GROUPS = 4
WINDOW = 128
GRID_W = 64
ROPE_BASE = 10000.0
EPS = 1e-6
NEG_INF = -1e30
LOG2_E = math.log2(math.e)
MOD_ROWS = 16
VMEM_LIMIT_BYTES = 56 * 1024 * 1024


def _cparams(*sem):
    return pltpu.CompilerParams(dimension_semantics=sem, vmem_limit_bytes=VMEM_LIMIT_BYTES)


def _dot(a, b):
    return jnp.dot(a, b, preferred_element_type=F32)


def _ln(v):
    mu = jnp.mean(v, axis=-1, keepdims=True)
    d = v - mu
    var = jnp.mean(d * d, axis=-1, keepdims=True)
    return d * lax.rsqrt(var + EPS)


def _post_norm(x, y, gate, lg, lb, alpha):
    return _ln(alpha * x + gate * y) * lg + lb


def _modulate(x, sh, sc):
    return _ln(x) * (1.0 + sc) + sh


def _mod_kernel(c_ref, w_ref, b_ref, o_ref):
    a = c_ref[...]
    a = a * jax.nn.sigmoid(a)
    a_hi = a.astype(BF16)
    a_lo = (a - a_hi.astype(F32)).astype(BF16)
    w = w_ref[0]
    w_hi = w.astype(BF16)
    w_lo = (w - w_hi.astype(F32)).astype(BF16)
    both = _dot(jnp.concatenate([a_hi, a_lo], axis=0), w_hi)
    o_ref[0] = both[:MOD_ROWS] + both[MOD_ROWS:] + _dot(a_hi, w_lo) + b_ref[0]


def _mod_all(cc, w_mod, b_mod, tn=1024):
    depth, d, n = w_mod.shape
    return pl.pallas_call(
        _mod_kernel,
        grid=(depth, n // tn),
        in_specs=[pl.BlockSpec((MOD_ROWS, d), lambda l, j: (0, 0)),
                  pl.BlockSpec((1, d, tn), lambda l, j: (l, 0, j)),
                  pl.BlockSpec((1, 1, tn), lambda l, j: (l, 0, j))],
        out_specs=pl.BlockSpec((1, MOD_ROWS, tn), lambda l, j: (l, 0, j)),
        out_shape=jax.ShapeDtypeStruct((depth, MOD_ROWS, n), F32),
        compiler_params=_cparams("parallel", "parallel"),
        name="mod_all",
    )(cc, w_mod, b_mod.reshape(depth, 1, n))


def _fold_kernel(cs_ref, w_ref, o_ref):
    o_ref[0] = jnp.dot(cs_ref[...], w_ref[0], preferred_element_type=F32,
                       precision=lax.Precision.HIGHEST)


def _fourier_fold(w_fourier):
    depth, groups, c, _ = w_fourier.shape
    k = np.arange(c)
    ang = 2.0 * np.pi * np.outer(k, k) / c
    cs = np.concatenate([np.cos(ang), -np.sin(ang)], axis=0) / math.sqrt(c)
    out = pl.pallas_call(
        _fold_kernel,
        grid=(depth * groups,),
        in_specs=[pl.BlockSpec((2 * c, c), lambda i: (0, 0)),
                  pl.BlockSpec((1, c, c), lambda i: (i, 0, 0))],
        out_specs=pl.BlockSpec((1, 2 * c, c), lambda i: (i, 0, 0)),
        out_shape=jax.ShapeDtypeStruct((depth * groups, 2 * c, c), F32),
        compiler_params=_cparams("parallel"),
        name="fourier_fold",
    )(jnp.asarray(cs, F32), w_fourier.reshape(depth * groups, c, c))
    out = out.reshape(depth, groups, 2, c, c).transpose(0, 1, 3, 2, 4).reshape(depth, groups, c, 2 * c)
    return out.astype(BF16)


def _ln_mod_kernel(x_ref, c_ref, sh_ref, sc_ref, h_ref, *, n_lat_tiles):
    latent = pl.program_id(0) < n_lat_tiles

    @pl.when(latent)
    def _():
        h_ref[...] = _modulate(x_ref[...], sh_ref[0], sc_ref[0]).astype(h_ref.dtype)

    @pl.when(jnp.logical_not(latent))
    def _():
        h_ref[...] = _modulate(c_ref[...], sh_ref[0], sc_ref[0]).astype(h_ref.dtype)


def _mod_spec(d, chunk, row_fn):
    return pl.BlockSpec((1, 1, d), lambda i, *_: (row_fn(i), 0, chunk))


def _ln_mod(x_lat, x_ctx, mod3, row_fn, tm):
    n_lat, d = x_lat.shape
    n_ctx = x_ctx.shape[0]
    n_lat_tiles, n = n_lat // tm, n_lat + n_ctx
    return pl.pallas_call(
        functools.partial(_ln_mod_kernel, n_lat_tiles=n_lat_tiles),
        grid=(n // tm,),
        in_specs=[pl.BlockSpec((tm, d), lambda i: (jnp.minimum(i, n_lat_tiles - 1), 0)),
                  pl.BlockSpec((tm, d), lambda i: (jnp.maximum(i - n_lat_tiles, 0), 0)),
                  _mod_spec(d, 0, row_fn), _mod_spec(d, 1, row_fn)],
        out_specs=pl.BlockSpec((tm, d), lambda i: (i, 0)),
        out_shape=jax.ShapeDtypeStruct((n, d), BF16),
        compiler_params=_cparams("parallel"),
        name="ln_mod",
    )(x_lat, x_ctx, mod3, mod3)


def _rope(t, cos, s1, s2):
    return t * cos + pltpu.roll(t, HEAD_DIM - 32, 1) * s1 + pltpu.roll(t, 32, 1) * s2


def _rms(t, gain):
    return t * lax.rsqrt(jnp.mean(t * t, axis=-1, keepdims=True) + EPS) * gain


def _win_kernel(h_ref, w_ref, qg_ref, kg_ref, cos_ref, s1_ref, s2_ref, p_ref, vt_ref, *,
                kinds, tm, tn, scale):
    h = h_ref[...]
    cos, s1, s2 = cos_ref[...], s1_ref[...], s2_ref[...]
    per = tn // HEAD_DIM
    n_v = 0
    for b in range(len(kinds) // per):
        acc = _dot(h, w_ref[:, b * tn:(b + 1) * tn])
        for c in range(per):
            kind = kinds[b * per + c]
            col = (b * per + c) * HEAD_DIM
            t = acc[:, c * HEAD_DIM:(c + 1) * HEAD_DIM]
            if kind == "qa":
                t = _rope(_rms(t, qg_ref[...]), cos, s1, s2) * scale
            elif kind == "qb":
                t = _rope(t, cos, s1, s2) * scale
            elif kind == "ka":
                t = _rope(_rms(t, kg_ref[...]), cos, s1, s2)
            elif kind == "kb":
                t = _rope(t, cos, s1, s2)
            p_ref[:, col:col + HEAD_DIM] = t.astype(p_ref.dtype)
            if kind == "v":
                for r in range(tm // HEAD_DIM):
                    tile = t[r * HEAD_DIM:(r + 1) * HEAD_DIM]
                    vt_ref[r, n_v * HEAD_DIM:(n_v + 1) * HEAD_DIM, :] = tile.T.astype(vt_ref.dtype)
                n_v += 1


def _in_proj(h, w, qg, kg, cos, s1, s2, *, layer, kinds, rope_fn, tm, tn, scale):
    n, d = h.shape
    in_w = w.shape[2]
    n_v = kinds.count("v")
    kern = functools.partial(_win_kernel, kinds=kinds, tm=tm, tn=tn, scale=scale)
    tab = pl.BlockSpec((tm, HEAD_DIM), lambda i: (rope_fn(i), 0))
    vec = pl.BlockSpec((1, HEAD_DIM), lambda i: (0, 0))
    return pl.pallas_call(
        kern,
        grid=(n // tm,),
        in_specs=[pl.BlockSpec((tm, d), lambda i: (i, 0)),
                  pl.BlockSpec((None, d, in_w), lambda i: (layer, 0, 0)),
                  vec, vec, tab, tab, tab],
        out_specs=[pl.BlockSpec((tm, in_w), lambda i: (i, 0)),
                   pl.BlockSpec((tm // HEAD_DIM, n_v * HEAD_DIM, HEAD_DIM), lambda i: (i, 0, 0))],
        out_shape=[jax.ShapeDtypeStruct((n, in_w), BF16),
                   jax.ShapeDtypeStruct((n // HEAD_DIM, n_v * HEAD_DIM, HEAD_DIM), BF16)],
        compiler_params=_cparams("parallel"),
        name="in_proj",
    )(h, w, qg, kg, cos, s1, s2)


def _scores_t(k, q):
    return lax.dot_general(k, q, (((1,), (1,)), ((), ())), preferred_element_type=F32)


def _vt_tile(vt_ref, first, count):
    tiles = [vt_ref[first + t] for t in range(count)]
    return tiles[0] if count == 1 else jnp.concatenate(tiles, axis=1)


def _online_first(s, vt):
    m = jnp.max(s, axis=0, keepdims=True)
    p = jnp.exp2(s - m)
    return m, jnp.sum(p, axis=0, keepdims=True), _dot(vt, p.astype(BF16))


def _exact_zero(acc):
    bits = pltpu.bitcast(acc[0:1, 0:HEAD_DIM], jnp.uint32)
    bits = lax.shift_right_logical(lax.shift_right_logical(bits, jnp.uint32(16)), jnp.uint32(16))
    return pltpu.bitcast(bits, F32).astype(BF16)


SUM_ROWS = 16


def _with_ones(vt):
    return jnp.concatenate([vt, jnp.ones((SUM_ROWS, vt.shape[1]), vt.dtype)], axis=0)


def _online_step(s, vt_ones, m, acc):
    m_new = jnp.maximum(m, jnp.max(s, axis=0, keepdims=True))
    p = jnp.exp2(s - m_new).astype(BF16)
    return m_new, jnp.exp2(m - m_new) * acc + _dot(vt_ones, p)


def _global_attn_kernel(*refs, group, tk, n_chunks, ctx_len):
    if n_chunks:
        q_ref, k_ref, vt_ref, kc_ref, vtc_ref, o_ref = refs[:6]
        s_refs = refs[6:6 + 2 * group]
        acc_refs = refs[6 + 2 * group:]
    else:
        q_ref, kc_ref, vtc_ref, o_ref = refs
    sub = tk // HEAD_DIM
    tq = q_ref.shape[0]
    cols = [slice(g * HEAD_DIM, (g + 1) * HEAD_DIM) for g in range(group)]
    qs = [q_ref[:, c] for c in cols]
    kc, vtc = kc_ref[...], _vt_tile(vtc_ref, 0, ctx_len // HEAD_DIM)
    if not n_chunks:
        for c, q in zip(cols, qs):
            _, l, acc = _online_first(_scores_t(kc, q), vtc)
            o_ref[:, c] = (acc / l).T.astype(o_ref.dtype)
    else:
        assert n_chunks % 2 == 0

        def lat_keys(c):
            return k_ref[pl.ds(pl.multiple_of(c * tk, tk), tk), :]

        def lat_values(c):
            return _with_ones(_vt_tile(vt_ref, c * sub, sub))

        def stage(k_next, slot_next, vt_cur, slot_cur, ms):
            out, gate = [], None
            for g, q in enumerate(qs):
                if k_next is not None:
                    n_keys = k_next.shape[0]
                    s_refs[slot_next * group + g][0:n_keys, :] = _scores_t(
                        k_next if gate is None else k_next + gate, q)
                if vt_cur is not None:
                    n_keys = vt_cur.shape[1]
                    m_new, acc = _online_step(s_refs[slot_cur * group + g][0:n_keys, :], vt_cur, ms[g],
                                              acc_refs[g][...])
                    acc_refs[g][...] = acc
                    gate = _exact_zero(m_new)
                    out.append(m_new)
            return tuple(out)

        def body(i, ms):
            ms = stage(lat_keys(2 * i + 1), 1, lat_values(2 * i), 0, ms)
            return stage(lat_keys(2 * i + 2), 0, lat_values(2 * i + 1), 1, ms)

        for acc_ref in acc_refs:
            acc_ref[...] = jnp.zeros_like(acc_ref)
        ms = (jnp.full((1, tq), -jnp.inf, F32),) * group
        stage(lat_keys(0), 0, None, None, None)
        ms = lax.fori_loop(0, n_chunks // 2 - 1, body, ms)
        ms = stage(lat_keys(n_chunks - 1), 1, lat_values(n_chunks - 2), 0, ms)
        ms = stage(kc, 0, lat_values(n_chunks - 1), 1, ms)
        stage(None, None, _with_ones(vtc), 0, ms)
        for c, acc_ref in zip(cols, acc_refs):
            o = acc_ref[0:HEAD_DIM, :] / acc_ref[HEAD_DIM:HEAD_DIM + 1, :]
            o_ref[:, c] = o.T.astype(o_ref.dtype)


def _window_attn_kernel(*refs, group, tq, seq, ctx_len, local):
    if local:
        sink_ref, bias_ref, q_ref, k_ref, vt_ref, kc_ref, vtc_ref, o_ref = refs
    else:
        sink_ref, q_ref, kc_ref, vtc_ref, o_ref = refs
    kv = pl.program_id(1)
    qi = pl.program_id(2)
    sub = tq // HEAD_DIM
    n_sub = seq // HEAD_DIM
    kc, vtc = kc_ref[...], _vt_tile(vtc_ref, 0, ctx_len // HEAD_DIM)
    if local:
        first = qi * sub
        prev = jnp.maximum(first - 1, 0)
        nxt = jnp.minimum(first + sub, n_sub - 1)
        k_loc = jnp.concatenate(
            [k_ref[pl.ds(pl.multiple_of(prev * HEAD_DIM, HEAD_DIM), HEAD_DIM), :],
             k_ref[pl.ds(pl.multiple_of(first * HEAD_DIM, tq), tq), :],
             k_ref[pl.ds(pl.multiple_of(nxt * HEAD_DIM, HEAD_DIM), HEAD_DIM), :]], axis=0)
        vt_loc = jnp.concatenate(
            [vt_ref[prev]] + [vt_ref[first + t] for t in range(sub)] + [vt_ref[nxt]], axis=1)
        bias = bias_ref[...]

    for g in range(group):
        cols = slice(g * HEAD_DIM, (g + 1) * HEAD_DIM)
        q = q_ref[:, cols]
        sink = sink_ref[kv * group + g] * LOG2_E
        s_c = _scores_t(kc, q)
        m = jnp.maximum(jnp.max(s_c, axis=0, keepdims=True), sink)
        if local:
            s_l = _scores_t(k_loc, q) + bias
            m = jnp.maximum(m, jnp.max(s_l, axis=0, keepdims=True))
        p_c = jnp.exp2(s_c - m)
        denom = jnp.sum(p_c, axis=0, keepdims=True) + jnp.exp2(sink - m)
        acc = _dot(vtc, p_c.astype(BF16))
        if local:
            p_l = jnp.exp2(s_l - m)
            denom = denom + jnp.sum(p_l, axis=0, keepdims=True)
            acc = acc + _dot(vt_loc, p_l.astype(BF16))
        o_ref[:, cols] = (acc / denom).T.astype(o_ref.dtype)


def _window_bias(n_loc, tq):
    key = jnp.arange(n_loc)[:, None] - HEAD_DIM
    band = jnp.abs(key - jnp.arange(tq)[None, :]) <= WINDOW
    variants = [band & (key >= 0), band, band & (key < tq)]
    return jnp.stack([jnp.where(v, 0.0, NEG_INF).astype(F32) for v in variants])


def _attention(p, vt3, out_prev, sink, *, window, latent_queries, batch, seq, ctx_len, q_off, k_off,
               v_row, group, out_width, tq, tk):
    n_all = p.shape[0]
    gw = group * HEAD_DIM
    ctx_blk0 = batch * seq // ctx_len
    kc_spec = pl.BlockSpec((ctx_len, HEAD_DIM), lambda b, kv, qi, *_: (ctx_blk0 + b, k_off // HEAD_DIM + kv))
    vtc_spec = pl.BlockSpec((ctx_len // HEAD_DIM, HEAD_DIM, HEAD_DIM),
                            lambda b, kv, qi, *_: (ctx_blk0 + b, v_row + kv, 0))
    if latent_queries:
        n_q = seq // tq
        q_blk = lambda b, kv, qi, *_: (b * n_q + qi, q_off // gw + kv)
        lat_specs = [pl.BlockSpec((seq, HEAD_DIM), lambda b, kv, qi, *_: (b, k_off // HEAD_DIM + kv)),
                     pl.BlockSpec((seq // HEAD_DIM, HEAD_DIM, HEAD_DIM), lambda b, kv, qi, *_: (b, v_row + kv, 0))]
        lat_args = [p, vt3]
    else:
        tq = ctx_len
        n_q = 1
        q_blk = lambda b, kv, qi, *_: (ctx_blk0 + b, q_off // gw + kv)
        lat_specs, lat_args = [], []
    q_spec = pl.BlockSpec((tq, gw), q_blk)
    o_spec = pl.BlockSpec((tq, gw), lambda b, kv, qi, *_: (q_blk(b, kv, qi)[0], kv))
    if window:
        kern = functools.partial(_window_attn_kernel, group=group, tq=tq, seq=seq, ctx_len=ctx_len,
                                 local=latent_queries)
        pre_specs = [pl.BlockSpec(memory_space=pltpu.SMEM)]
        pre_args = [sink]
        if latent_queries:
            assert n_q >= 2
            n_loc = tq + 2 * HEAD_DIM
            pre_specs.append(pl.BlockSpec(
                (None, n_loc, tq), lambda b, kv, qi: (jnp.where(qi == 0, 0, jnp.where(qi == n_q - 1, 2, 1)), 0, 0)))
            pre_args.append(_window_bias(n_loc, tq))
    else:
        kern = functools.partial(_global_attn_kernel, group=group, tk=tk,
                                 n_chunks=seq // tk if latent_queries else 0, ctx_len=ctx_len)
        pre_specs, pre_args = [], []
    scratch = []
    if latent_queries and not window:
        assert ctx_len <= tk
        scratch = [pltpu.VMEM((tk, tq), F32)] * (2 * group) + [pltpu.VMEM((HEAD_DIM + SUM_ROWS, tq), F32)] * group
    in_specs = pre_specs + [q_spec] + lat_specs + [kc_spec, vtc_spec]
    args = pre_args + [p] + lat_args + [p, vt3]
    aliases = {}
    if out_prev is not None:
        in_specs.append(pl.BlockSpec(memory_space=pl.ANY))
        args.append(out_prev)
        aliases = {len(args) - 1: 0}
        body = kern

        def kern(*refs):
            return body(*refs[:-2], refs[-1])
    return pl.pallas_call(
        kern,
        grid=(batch, A_KV_HEADS, n_q),
        in_specs=in_specs,
        out_specs=o_spec,
        out_shape=jax.ShapeDtypeStruct((n_all, out_width), BF16),
        input_output_aliases=aliases,
        scratch_shapes=scratch,
        compiler_params=_cparams("parallel", "parallel", "arbitrary"),
        name=("window" if window else "global") + ("_lat" if latent_queries else "_ctx"),
    )(*args)


def _chan_kernel(u_ref, w_ref, v_ref):
    groups, c, _ = w_ref.shape
    for g in range(groups):
        cols = slice(g * c, (g + 1) * c)
        r = _dot(u_ref[:, cols], w_ref[g])
        v_ref[0, :, cols] = r[:, :c].astype(v_ref.dtype)
        v_ref[1, :, cols] = r[:, c:].astype(v_ref.dtype)


def _fourier_channels(p, wfold_l, f_off, tm):
    n = p.shape[0]
    groups, c, _ = wfold_l.shape
    f_w = groups * c
    assert f_off % f_w == 0
    return pl.pallas_call(
        _chan_kernel,
        grid=(n // tm,),
        in_specs=[pl.BlockSpec((tm, f_w), lambda i: (i, f_off // f_w)),
                  pl.BlockSpec((groups, c, 2 * c), lambda i: (0, 0, 0))],
        out_specs=pl.BlockSpec((2, tm, f_w), lambda i: (0, i, 0)),
        out_shape=jax.ShapeDtypeStruct((2, n, f_w), BF16),
        compiler_params=_cparams("parallel"),
        name="fourier_channels",
    )(p, wfold_l)


def _dft_kernel(w_ref, m_ref, v_ref, o_ref, y_ref, *, per):
    _, n1, n2, cols = y_ref.shape
    tn = per * cols
    for j in range(n2 // per):
        lanes = slice(j * tn, (j + 1) * tn)
        x = jnp.concatenate([v_ref[0, :, lanes], v_ref[1, :, lanes]], axis=0)
        y = _dot(w_ref[...], x)
        for part in range(2):
            for e in range(per):
                y_ref[part, :, j * per + e, :] = y[part * n1:(part + 1) * n1, e * cols:(e + 1) * cols]
    for k1 in range(n1):
        x = jnp.concatenate([y_ref[0, k1], y_ref[1, k1]], axis=0).astype(BF16)
        o_ref[:, k1, :] = _dot(m_ref[k1], x)


def _dense_dft_kernel(m_ref, v_ref, prev_ref, o_ref):
    del prev_ref
    x = jnp.concatenate([v_ref[0], v_ref[1]], axis=0)
    o_ref[...] = _dot(m_ref[...], x).astype(o_ref.dtype)


def _dft_tables(seq, ctx_len):
    n1 = n2 = int(round(math.sqrt(seq)))
    assert n1 * n2 == seq
    a1 = 2.0 * np.pi * np.outer(np.arange(n1), np.arange(n1)) / n1
    c1, s1 = np.cos(a1), np.sin(a1)
    w1 = np.block([[c1, s1], [-s1, c1]])
    k = np.arange(n1)[:, None, None] + n1 * np.arange(n2)[None, :, None]
    a2 = 2.0 * np.pi * ((k * np.arange(n2)[None, None, :]) % seq) / seq
    m2 = np.concatenate([np.cos(a2), np.sin(a2)], axis=2) / math.sqrt(seq)
    ac = 2.0 * np.pi * np.outer(np.arange(ctx_len), np.arange(ctx_len)) / ctx_len
    mc = np.concatenate([np.cos(ac), np.sin(ac)], axis=1) / math.sqrt(ctx_len)
    return n1, n2, jnp.asarray(w1, BF16), jnp.asarray(m2, BF16), jnp.asarray(mc, BF16)


def _fourier_positions(v, tables, *, batch, seq, ctx_len, with_ctx, per=8):
    n1, n2, w1, m2, mc = tables
    _, n_all, cols = v.shape
    v_lat = v.reshape(2, n_all // n2, n2 * cols)
    o = pl.pallas_call(
        functools.partial(_dft_kernel, per=min(per, n2)),
        grid=(batch,),
        in_specs=[pl.BlockSpec((2 * n1, 2 * n1), lambda b: (0, 0)),
                  pl.BlockSpec((n1, n2, 2 * n2), lambda b: (0, 0, 0)),
                  pl.BlockSpec((2, n1, n2 * cols), lambda b: (0, b, 0))],
        out_specs=pl.BlockSpec((n2, n1, cols), lambda b: (b, 0, 0)),
        out_shape=jax.ShapeDtypeStruct((n_all // n1, n1, cols), F32),
        scratch_shapes=[pltpu.VMEM((2, n1, n2, cols), F32)],
        compiler_params=_cparams("parallel"),
        name="dft_latent",
    )(w1, m2, v_lat)
    o = o.reshape(n_all, cols)
    if not with_ctx:
        return o
    blk0 = batch * seq // ctx_len
    return pl.pallas_call(
        _dense_dft_kernel,
        grid=(batch,),
        in_specs=[pl.BlockSpec((ctx_len, 2 * ctx_len), lambda b: (0, 0)),
                  pl.BlockSpec((2, ctx_len, cols), lambda b: (0, blk0 + b, 0)),
                  pl.BlockSpec(memory_space=pl.ANY)],
        out_specs=pl.BlockSpec((ctx_len, cols), lambda b: (blk0 + b, 0)),
        out_shape=jax.ShapeDtypeStruct((n_all, cols), F32),
        input_output_aliases={2: 0},
        compiler_params=_cparams("parallel"),
        name="dft_ctx",
    )(mc, v, o)


def _out_proj_kernel(oa_ref, ob_ref, of_ref, w_ref, *rest, alpha, n_lat_tiles=None, n_chunks=4):
    i = pl.program_id(0)
    if n_lat_tiles is None:
        x_ref, g_ref, lg_ref, lb_ref, sh_ref, sc_ref, xo_ref, ho_ref, y0_ref, y1_ref = rest
        residual = lambda rows: x_ref[rows, :]
    else:
        xl_ref, xc_ref, g_ref, lg_ref, lb_ref, sh_ref, sc_ref, xo_ref, ho_ref, y0_ref, y1_ref = rest
        latent = jnp.maximum(i - 1, 0) < n_lat_tiles
        residual = lambda rows: jnp.where(latent, xl_ref[rows, :], xc_ref[rows, :])

    @pl.when(i == 0)
    def _():
        y1_ref[...] = jnp.zeros_like(y1_ref)

    def step(y_write, y_read):
        tm, d = y_write.shape
        cw, rc = d // n_chunks, tm // n_chunks
        oa, ob, of = oa_ref[...], ob_ref[...], of_ref[...].astype(BF16)
        gate = None
        for c in range(n_chunks):
            ob_c = ob if gate is None else ob + jnp.concatenate([gate] * (ob.shape[1] // HEAD_DIM), axis=1)
            mixed = jnp.concatenate([oa, ob_c, of], axis=1)
            y_write[:, c * cw:(c + 1) * cw] = _dot(mixed, w_ref[:, c * cw:(c + 1) * cw])
            rows = slice(c * rc, (c + 1) * rc)
            xn = _post_norm(residual(rows), y_read[rows, :], g_ref[0], lg_ref[...], lb_ref[...], alpha)
            xo_ref[rows, :] = xn
            dev = xn - jnp.mean(xn, axis=-1, keepdims=True)
            var = jnp.mean(dev * dev, axis=-1, keepdims=True)
            h = dev * lax.rsqrt(var + EPS) * (1.0 + sc_ref[0]) + sh_ref[0]
            ho_ref[rows, :] = h.astype(ho_ref.dtype)
            gate = _exact_zero(jnp.broadcast_to(jnp.sum(var, axis=0, keepdims=True), (1, HEAD_DIM)))

    @pl.when(i % 2 == 0)
    def _():
        step(y0_ref, y1_ref)

    @pl.when(i % 2 == 1)
    def _():
        step(y1_ref, y0_ref)


def _out_proj(oa, ob, of, w_out, x_res, mod3, lg, lb, *, layer, row_fn, n_rows, tm, alpha):
    n_all = oa.shape[0]
    wa, wb, wf = oa.shape[1], ob.shape[1], of.shape[1]
    d = w_out.shape[2]
    assert wa + wb + wf == w_out.shape[1]
    n_tiles = n_rows // tm
    lead = lambda i: (jnp.minimum(i, n_tiles - 1), 0)
    lag = lambda i: (jnp.maximum(i - 1, 0), 0)
    lag_row = lambda i: row_fn(jnp.maximum(i - 1, 0))
    vec = pl.BlockSpec((1, d), lambda i: (0, 0))
    if isinstance(x_res, tuple):
        n_lat_tiles = x_res[0].shape[0] // tm
        x_args = list(x_res)
        x_specs = [pl.BlockSpec((tm, d), lambda i: (jnp.minimum(lag(i)[0], n_lat_tiles - 1), 0)),
                   pl.BlockSpec((tm, d), lambda i: (jnp.maximum(lag(i)[0] - n_lat_tiles, 0), 0))]
    else:
        n_lat_tiles, x_args, x_specs = None, [x_res], [pl.BlockSpec((tm, d), lag)]
    return pl.pallas_call(
        functools.partial(_out_proj_kernel, alpha=alpha, n_lat_tiles=n_lat_tiles),
        grid=(n_tiles + 1,),
        in_specs=[pl.BlockSpec((tm, wa), lead), pl.BlockSpec((tm, wb), lead), pl.BlockSpec((tm, wf), lead),
                  pl.BlockSpec((None,) + w_out.shape[1:], lambda i: (layer, 0, 0))] + x_specs + [
                  _mod_spec(d, 2, lag_row), vec, vec, _mod_spec(d, 3, lag_row), _mod_spec(d, 4, lag_row)],
        out_specs=[pl.BlockSpec((tm, d), lag), pl.BlockSpec((tm, d), lag)],
        out_shape=[jax.ShapeDtypeStruct((n_all, d), F32), jax.ShapeDtypeStruct((n_all, d), BF16)],
        scratch_shapes=[pltpu.VMEM((tm, d), F32), pltpu.VMEM((tm, d), F32)],
        compiler_params=_cparams("arbitrary"),
        name="out_proj",
    )(oa, ob, of, w_out, *x_args, mod3, lg, lb, mod3, mod3)


HALO = 16
FFN_EPILOGUE_STEPS = 8


def _ffn_kernel(hp_ref, hm_ref, hn_ref, wu_ref, wg_ref, cw_ref, cb_ref, wd_ref, x_ref, g_ref, lg_ref,
                lb_ref, *rest, tm, n_tiles, n_lat_tiles, seq, ctx_len, alpha, emit_h, n_epi):
    if emit_h:
        sh_ref, sc_ref, xo_ref, ho_ref, hext_ref, acc_ref = rest
    else:
        xo_ref, hext_ref, acc_ref = rest
    i = pl.program_id(0)
    j = pl.program_id(1)
    slot = i % 2
    rc = tm // n_epi

    def epilogue_chunk():
        rows = pl.ds(pl.multiple_of(j * rc, rc), rc)
        xn = _post_norm(x_ref[rows, :], acc_ref[1 - slot, rows, :], g_ref[0], lg_ref[...], lb_ref[...], alpha)
        xo_ref[rows, :] = xn
        if emit_h:
            ho_ref[rows, :] = _modulate(xn, sh_ref[0], sc_ref[0]).astype(ho_ref.dtype)

    def main():
        tf = wu_ref.shape[1]
        period = jnp.where(i < n_lat_tiles, seq, ctx_len)
        pos = (i * tm + lax.broadcasted_iota(jnp.int32, (tm, tf), 0)) & (period - 1)
        gext = _dot(hext_ref[...], wg_ref[...])
        u = _dot(hm_ref[...], wu_ref[...])
        g_prev = jnp.where(pos == 0, 0.0, gext[HALO - 1:HALO - 1 + tm])
        g_cur = gext[HALO:HALO + tm]
        g_next = jnp.where(pos == period - 1, 0.0, gext[HALO + 1:HALO + 1 + tm])
        cw = cw_ref[...]
        gc = cb_ref[...] + cw[0:1] * g_prev + cw[1:2] * g_cur + cw[2:3] * g_next
        act = (gc * jax.nn.sigmoid(gc) * u).astype(BF16)
        acc_ref[slot] += _dot(act, wd_ref[...])

    active = i < n_tiles

    @pl.when(jnp.logical_and(active, j == 0))
    def _():
        hext_ref[0:HALO] = hp_ref[...]
        hext_ref[HALO:HALO + tm] = hm_ref[...]
        hext_ref[HALO + tm:] = hn_ref[...]
        acc_ref[slot] = jnp.zeros((tm, acc_ref.shape[2]), F32)

    @pl.when(jnp.logical_and(i == 0, j == 0))
    def _():
        acc_ref[1] = jnp.zeros((tm, acc_ref.shape[2]), F32)

    @pl.when(jnp.logical_and(active, j < n_epi))
    def _():
        epilogue_chunk()
        main()

    @pl.when(jnp.logical_and(active, j >= n_epi))
    def _():
        main()

    @pl.when(jnp.logical_and(jnp.logical_not(active), j < n_epi))
    def _():
        epilogue_chunk()


def _conv_ffn(h, w_up, w_gate, conv_w, conv_b, w_down, x_all, mod3, lg, lb, mod3_next, *, layer, row_fn,
              n_rows, n_out_rows, tm, tf, seq, ctx_len, n_lat_tiles, alpha):
    n_all, d = x_all.shape
    d_ff = w_up.shape[2]
    emit_h = mod3_next is not None
    per = tm // HALO
    last_halo = n_rows // HALO - 1
    n_tiles, nj = n_rows // tm, d_ff // tf
    n_epi = min(FFN_EPILOGUE_STEPS, nj)
    lead = lambda i: jnp.minimum(i, n_tiles - 1)
    lag = lambda i: jnp.maximum(i - 1, 0)
    col = lambda i, j: jnp.where(i < n_tiles, j, nj - 1)
    lag_row = lambda i: row_fn(lag(i))
    vec = pl.BlockSpec((1, d), lambda i, j: (0, 0))
    in_specs = [pl.BlockSpec((HALO, d), lambda i, j: (jnp.maximum(lead(i) * per - 1, 0), 0)),
                pl.BlockSpec((tm, d), lambda i, j: (lead(i), 0)),
                pl.BlockSpec((HALO, d), lambda i, j: (jnp.minimum((lead(i) + 1) * per, last_halo), 0)),
                pl.BlockSpec((None, d, tf), lambda i, j: (layer, 0, col(i, j))),
                pl.BlockSpec((None, d, tf), lambda i, j: (layer, 0, col(i, j))),
                pl.BlockSpec((conv_w.shape[0], tf), lambda i, j: (0, col(i, j))),
                pl.BlockSpec((1, tf), lambda i, j: (0, col(i, j))),
                pl.BlockSpec((None, tf, d), lambda i, j: (layer, col(i, j), 0)),
                pl.BlockSpec((tm, d), lambda i, j: (lag(i), 0)),
                _mod_spec(d, 5, lag_row), vec, vec]
    args = [h, h, h, w_up, w_gate, conv_w, conv_b, w_down, x_all, mod3, lg, lb]
    out_specs = [pl.BlockSpec((tm, d), lambda i, j: (lag(i), 0))]
    out_shape = [jax.ShapeDtypeStruct((n_out_rows, d), F32)]
    if emit_h:
        in_specs += [_mod_spec(d, 0, lag_row), _mod_spec(d, 1, lag_row)]
        args += [mod3_next, mod3_next]
        out_specs.append(pl.BlockSpec((tm, d), lambda i, j: (lag(i), 0)))
        out_shape.append(jax.ShapeDtypeStruct((n_out_rows, d), BF16))
    kern = functools.partial(_ffn_kernel, tm=tm, n_tiles=n_tiles, n_lat_tiles=n_lat_tiles, seq=seq,
                             ctx_len=ctx_len, alpha=alpha, emit_h=emit_h, n_epi=n_epi)
    return pl.pallas_call(
        kern,
        grid=(n_tiles + 1, nj),
        in_specs=in_specs,
        out_specs=out_specs,
        out_shape=out_shape,
        scratch_shapes=[pltpu.VMEM((tm + 2 * HALO, d), BF16), pltpu.VMEM((2, tm, d), F32)],
        compiler_params=_cparams("arbitrary", "arbitrary"),
        name="conv_ffn",
    )(*args)


def _rope_tables(seq, pad_rows):
    rows = seq // GRID_W
    row = jnp.repeat(jnp.arange(rows), GRID_W).astype(F32)
    col = jnp.tile(jnp.arange(GRID_W), rows).astype(F32)
    n_freq = HEAD_DIM // 4
    inv_freq = ROPE_BASE ** (-jnp.arange(n_freq, dtype=F32) / n_freq)
    ar, ac = row[:, None] * inv_freq, col[:, None] * inv_freq
    zero = jnp.zeros_like(ar)
    cos = jnp.concatenate([jnp.cos(ar), jnp.cos(ar), jnp.cos(ac), jnp.cos(ac)], axis=1)
    s1 = jnp.concatenate([-jnp.sin(ar), zero, -jnp.sin(ac), zero], axis=1)
    s2 = jnp.concatenate([zero, jnp.sin(ar), zero, jnp.sin(ac)], axis=1)
    pad = lambda t, v: jnp.concatenate([t, jnp.full((pad_rows, HEAD_DIM), v, F32)], axis=0)
    return pad(cos, 1.0), pad(s1, 0.0), pad(s2, 0.0)


def _forward(x, c, ctx, c_ctx, w_mod, b_mod, w_in, q_gain_a, k_gain_a, sink_b, w_fourier, w_out,
             ln1_g, ln1_b, w_up, w_gate, conv_w, conv_b, w_down, ln2_g, ln2_b,
             *, tm=512, tm_in=512, tm_out=256, tn_in=512, tf=512, tq=512, tk=512):
    batch, seq, d = x.shape
    ctx_len = ctx.shape[1]
    depth = w_mod.shape[0]
    n_lat, n_ctx = batch * seq, batch * ctx_len
    n_all = n_lat + n_ctx
    qa_w, qb_w, f_w = d // 2, d // 4, d // 4
    ka_w, kb_w = A_KV_HEADS * HEAD_DIM, B_KV_HEADS * HEAD_DIM
    assert f_w == F_GROUPS * HEAD_DIM and batch + 1 <= MOD_ROWS
    assert seq & (seq - 1) == 0 and ctx_len & (ctx_len - 1) == 0
    widths = (("qa", qa_w), ("qb", qb_w), ("ka", ka_w), ("v", ka_w), ("kb", kb_w), ("v", kb_w), ("f", f_w))
    kinds = tuple(kind for kind, w in widths for _ in range(w // HEAD_DIM))
    qb_off = qa_w
    ka_off = qa_w + qb_w
    kb_off = ka_off + 2 * ka_w
    f_off = kb_off + 2 * kb_w
    alpha = (2.0 * depth) ** 0.25
    scale = HEAD_DIM ** -0.5 * LOG2_E

    def row_fn_for(t):
        return lambda i: jnp.where(i < n_lat // t, i // (seq // t), batch)

    def rope_fn(i):
        return jnp.where(i < n_lat // tm_in, i % (seq // tm_in), seq // tm_in)

    cc = jnp.concatenate([c, c_ctx[None], jnp.zeros((MOD_ROWS - batch - 1, d), F32)], axis=0)
    mod = _mod_all(cc, w_mod, b_mod)
    wfold = _fourier_fold(w_fourier)
    cos, s1, s2 = _rope_tables(seq, tm_in)
    tables = _dft_tables(seq, ctx_len)
    mod3 = [mod[l].reshape(MOD_ROWS, 1, 6 * d) for l in range(depth)]
    vec = lambda t: t.reshape(1, -1)

    w_in_b, w_out_b = w_in.astype(BF16), w_out.astype(BF16)
    w_up_b, w_gate_b, w_down_b = w_up.astype(BF16), w_gate.astype(BF16), w_down.astype(BF16)

    x_cur = (x.reshape(n_lat, d), ctx.reshape(n_ctx, d))
    h = _ln_mod(*x_cur, mod3[0], row_fn_for(tm), tm)
    for l in range(depth):
        last = l == depth - 1
        p, vt3 = _in_proj(h, w_in_b, vec(q_gain_a[l]), vec(k_gain_a[l]), cos, s1, s2, layer=l, kinds=kinds,
                          rope_fn=rope_fn, tm=tm_in, tn=tn_in, scale=scale)
        common = dict(batch=batch, seq=seq, ctx_len=ctx_len, tq=tq, tk=tk)
        ga = dict(window=False, q_off=0, k_off=ka_off, v_row=0, group=qa_w // ka_w, out_width=qa_w, **common)
        gb = dict(window=True, q_off=qb_off, k_off=kb_off, v_row=A_KV_HEADS, group=qb_w // kb_w,
                  out_width=qb_w, **common)
        o_a = _attention(p, vt3, None, None, latent_queries=True, **ga)
        o_b = _attention(p, vt3, None, sink_b[l], latent_queries=True, **gb)
        v = _fourier_channels(p, wfold[l], f_off, 2 * tm if n_all % (2 * tm) == 0 else tm)
        if not last:
            o_a = _attention(p, vt3, o_a, None, latent_queries=False, **ga)
            o_b = _attention(p, vt3, o_b, sink_b[l], latent_queries=False, **gb)
        o_f = _fourier_positions(v, tables, batch=batch, seq=seq, ctx_len=ctx_len, with_ctx=not last)
        n_rows = n_lat if last else n_all
        x_mid, h_mid = _out_proj(o_a, o_b, o_f, w_out_b, x_cur, mod3[l], vec(ln1_g[l]), vec(ln1_b[l]),
                                 layer=l, row_fn=row_fn_for(tm_out), n_rows=n_rows, tm=tm_out, alpha=alpha)
        outs = _conv_ffn(h_mid, w_up_b, w_gate_b, conv_w[l], vec(conv_b[l]), w_down_b, x_mid,
                         mod3[l], vec(ln2_g[l]), vec(ln2_b[l]), None if last else mod3[l + 1],
                         layer=l, row_fn=row_fn_for(tm), n_rows=n_rows, n_out_rows=n_rows, tm=tm, tf=tf, seq=seq,
                         ctx_len=ctx_len, n_lat_tiles=n_lat // tm, alpha=alpha)
        if last:
            x_cur = outs[0]
        else:
            x_cur, h = outs
    return x_cur.reshape(batch, seq, d)


def kernel(x, c, ctx, c_ctx, w_mod, b_mod, w_in, q_gain_a, k_gain_a, sink_b, w_fourier, w_out, ln1_g, ln1_b,
           w_up, w_gate, conv_w, conv_b, w_down, ln2_g, ln2_b):
    return _forward(x, c, ctx, c_ctx, w_mod, b_mod, w_in, q_gain_a, k_gain_a, sink_b, w_fourier, w_out,
                    ln1_g, ln1_b, w_up, w_gate, conv_w, conv_b, w_down, ln2_g, ln2_b)
```
